```python
import math
import jax
import jax.numpy as jnp
from jax import lax
import numpy as np

D_MODEL = 1024
BATCH = 2
SEQ = 16384
DEPTH = 4

N_EVEN = (DEPTH + 1) // 2
N_ODD = DEPTH // 2
NORM_EPS = 1e-6

SSD_HEAD_DIM = 64
SSD_HEADS = 16
SSD_GROUPS = 2
SSD_HEADS_PER_GROUP = SSD_HEADS // SSD_GROUPS
SSD_STATE = 128
SSD_CHUNK = 128
CONV_WIDTH = 4
D_SSM = SSD_HEADS * SSD_HEAD_DIM
CONV_DIM = D_SSM + 2 * SSD_GROUPS * SSD_STATE

GLA_HEADS = 4
GLA_KEY_DIM = D_MODEL // 2
GLA_VAL_DIM = D_MODEL
GLA_HEAD_K = GLA_KEY_DIM // GLA_HEADS
GLA_HEAD_V = GLA_VAL_DIM // GLA_HEADS
GLA_GATE_RANK = 16
GLA_GATE_NORM = 16.0
GLA_CHUNK = 64

IN_SIZES = (D_SSM, CONV_DIM, SSD_HEADS, GLA_KEY_DIM, GLA_KEY_DIM, GLA_VAL_DIM, GLA_VAL_DIM, GLA_GATE_RANK)
IN_COLS = sum(IN_SIZES)
MIX_WIDTH = D_SSM + GLA_VAL_DIM

MLA_HEADS = 16
MLA_NOPE = 64
MLA_ROPE = 32
MLA_V = 64
MLA_Q_RANK = 384
MLA_KV_RANK = 256
ROPE_THETA = 10000.0
ATTN_BLOCK = 128

FFN_HIDDEN = -(-(8 * D_MODEL) // (3 * 256)) * 256

kernel_name = 'hybrid_ssd_gla_mla_trunk'


def _offsets(sizes):
    out, acc = [], 0
    for s in sizes[:-1]:
        acc += s
        out.append(acc)
    return out


def rms_norm(x, w):
    xf = x.astype(jnp.float32)
    y = xf * lax.rsqrt(jnp.mean(xf * xf, axis=-1, keepdims=True) + NORM_EPS)
    return (y * w.astype(jnp.float32)).astype(x.dtype)


def causal_depthwise_conv(x, w, b):
    c = x.shape[-1]
    y = lax.conv_general_dilated(x, w[:, None, :].astype(x.dtype), window_strides=(1,),
                                 padding=[(CONV_WIDTH - 1, 0)],
                                 dimension_numbers=('NWC', 'WIO', 'NWC'),
                                 feature_group_count=c)
    return y + b


def ssd_mixer(z, xbc, dt_raw, conv_w, conv_b, dt_bias, a_log, d_skip, norm_w):
    bsz, seqlen, _ = z.shape
    G, K, P, N, Q = SSD_GROUPS, SSD_HEADS_PER_GROUP, SSD_HEAD_DIM, SSD_STATE, SSD_CHUNK
    nc = seqlen // Q
    xbc = jax.nn.silu(causal_depthwise_conv(xbc, conv_w, conv_b))
    xs, bm, cm = jnp.split(xbc, [D_SSM, D_SSM + G * N], axis=-1)
    xs = xs.reshape(bsz, nc, Q, G, K, P)
    bm = bm.reshape(bsz, nc, Q, G, N)
    cm = cm.reshape(bsz, nc, Q, G, N)
    dt = jax.nn.softplus(dt_raw + dt_bias).reshape(bsz, nc, Q, G, K)
    a = dt * (-jnp.exp(a_log)).reshape(G, K)
    a_cum = jnp.cumsum(a, axis=2)
    xdt = xs * dt[..., None]
    causal = jnp.tril(jnp.ones((Q, Q), dtype=bool))
    seg = a_cum[:, :, :, None] - a_cum[:, :, None, :]
    decay = jnp.exp(jnp.where(causal[:, :, None, None], seg, -jnp.inf))
    cb = jnp.einsum('bclgn,bcsgn->bclsg', cm, bm)
    y_diag = jnp.einsum('bclsg,bclsgk,bcsgkp->bclgkp', cb, decay, xdt)
    to_end = jnp.exp(a_cum[:, :, -1:] - a_cum)
    states = jnp.einsum('bcsgn,bcsgk,bcsgkp->bcgkpn', bm, to_end, xdt)
    chunk_decay = jnp.exp(a_cum[:, :, -1])

    def step(h, inp):
        s_c, d_c = inp
        return h * d_c[..., None, None] + s_c, h

    h0 = jnp.zeros((bsz, G, K, P, N), states.dtype)
    _, prev = lax.scan(step, h0, (jnp.moveaxis(states, 1, 0), jnp.moveaxis(chunk_decay, 1, 0)))
    prev = jnp.moveaxis(prev, 0, 1)
    y_off = jnp.einsum('bclgn,bcgkpn,bclgk->bclgkp', cm, prev, jnp.exp(a_cum))
    y = y_diag + y_off + xs * d_skip.reshape(G, K)[:, :, None]
    y = y.reshape(bsz, seqlen, D_SSM) * jax.nn.silu(z)
    gsz = D_SSM // G
    y = rms_norm(y.reshape(bsz, seqlen, G, gsz), norm_w.reshape(G, gsz))
    return y.reshape(bsz, seqlen, D_SSM)


def gla_mixer(q, k, v, g, gk_low, gk_w2, gk_b, norm_w):
    bsz, seqlen, _ = q.shape
    H, DK, DV, Q = GLA_HEADS, GLA_HEAD_K, GLA_HEAD_V, GLA_CHUNK
    nc = seqlen // Q
    dtype = q.dtype
    gk = jax.nn.log_sigmoid((gk_low @ gk_w2 + gk_b).astype(jnp.float32)) / GLA_GATE_NORM
    gcum = jnp.cumsum(gk.reshape(bsz, nc, Q, H, DK), axis=2)
    g_last = gcum[:, :, -1]
    q = q.reshape(bsz, nc, Q, H, DK) * (DK ** -0.5)
    k = k.reshape(bsz, nc, Q, H, DK)
    v = v.reshape(bsz, nc, Q, H, DV)
    q_dec = q * jnp.exp(gcum).astype(dtype)
    k_inv = k * jnp.exp(-gcum).astype(dtype)
    k_end = k * jnp.exp(g_last[:, :, None] - gcum).astype(dtype)
    causal = jnp.tril(jnp.ones((Q, Q), dtype=bool))
    scores = jnp.where(causal, jnp.einsum('bcihd,bcjhd->bchij', q_dec, k_inv), 0)
    o_intra = jnp.einsum('bchij,bcjhv->bcihv', scores, v)
    chunk_kv = jnp.einsum('bcjhd,bcjhv->bchdv', k_end, v)
    chunk_decay = jnp.exp(g_last).astype(dtype)

    def step(s, inp):
        kv_c, d_c = inp
        return s * d_c[..., None] + kv_c, s

    s0 = jnp.zeros((bsz, H, DK, DV), chunk_kv.dtype)
    _, prev = lax.scan(step, s0, (jnp.moveaxis(chunk_kv, 1, 0), jnp.moveaxis(chunk_decay, 1, 0)))
    prev = jnp.moveaxis(prev, 0, 1)
    o_inter = jnp.einsum('bcihd,bchdv->bcihv', q_dec, prev)
    o = (o_intra + o_inter).reshape(bsz, seqlen, H, DV)
    o = rms_norm(o, norm_w) * jax.nn.silu(g.reshape(bsz, seqlen, H, DV))
    return o.reshape(bsz, seqlen, GLA_VAL_DIM)


def ssd_gla_layer(h, w_in, conv_w, conv_b, dt_bias, a_log, d_skip, ssd_norm,
                  gla_gk_w2, gla_gk_b, gla_norm, w_out):
    bsz, seqlen, _ = h.shape
    proj = h @ w_in
    z, xbc, dt_raw, q, k, v, g, gk_low = jnp.split(proj, _offsets(IN_SIZES), axis=-1)
    y_ssd = ssd_mixer(z, xbc, dt_raw, conv_w, conv_b, dt_bias, a_log, d_skip, ssd_norm)
    y_gla = gla_mixer(q, k, v, g, gk_low, gla_gk_w2, gla_gk_b, gla_norm)
    return jnp.concatenate([y_ssd, y_gla], axis=-1) @ w_out


def rope_tables(seqlen):
    pos = jnp.arange(seqlen, dtype=jnp.float32)
    inv = 1.0 / (ROPE_THETA ** (jnp.arange(0, MLA_ROPE, 2, dtype=jnp.float32) / MLA_ROPE))
    ang = pos[:, None] * inv[None, :]
    return jnp.cos(ang), jnp.sin(ang)


def apply_rope(x, cos, sin):
    x1, x2 = jnp.split(x, 2, axis=-1)
    cos = cos.astype(x.dtype)
    sin = sin.astype(x.dtype)
    return jnp.concatenate([x1 * cos - x2 * sin, x2 * cos + x1 * sin], axis=-1)


def mla_layer(h, w_dqkv, q_lora_norm, w_uq, kv_lora_norm, w_ukv,
              q_nope_norm, q_rope_norm, k_nope_norm, k_rope_norm, w_o):
    bsz, seqlen, _ = h.shape
    H = MLA_HEADS
    cq, ckv, k_rope = jnp.split(h @ w_dqkv, [MLA_Q_RANK, MLA_Q_RANK + MLA_KV_RANK], axis=-1)
    q = (rms_norm(cq, q_lora_norm) @ w_uq).reshape(bsz, seqlen, H, MLA_NOPE + MLA_ROPE)
    q_nope, q_rope = jnp.split(q, [MLA_NOPE], axis=-1)
    kv = (rms_norm(ckv, kv_lora_norm) @ w_ukv).reshape(bsz, seqlen, H, MLA_NOPE + MLA_V)
    k_nope, v = jnp.split(kv, [MLA_NOPE], axis=-1)
    cos, sin = rope_tables(seqlen)
    q_nope = rms_norm(q_nope, q_nope_norm)
    q_rope = apply_rope(rms_norm(q_rope, q_rope_norm), cos[:, None, :], sin[:, None, :])
    k_nope = rms_norm(k_nope, k_nope_norm)
    k_rope = apply_rope(rms_norm(k_rope, k_rope_norm), cos, sin)
    scale = (MLA_NOPE + MLA_ROPE) ** -0.5
    kpos = jnp.arange(seqlen)

    def block(i):
        start = i * ATTN_BLOCK
        qn = lax.dynamic_slice_in_dim(q_nope, start, ATTN_BLOCK, axis=1)
        qr = lax.dynamic_slice_in_dim(q_rope, start, ATTN_BLOCK, axis=1)
        s = jnp.einsum('bqhd,bkhd->bhqk', qn, k_nope) + jnp.einsum('bqhr,bkr->bhqk', qr, k_rope)
        s = s.astype(jnp.float32) * scale
        qpos = start + jnp.arange(ATTN_BLOCK)
        mask = kpos[None, :] <= qpos[:, None]
        p = jax.nn.softmax(jnp.where(mask, s, -jnp.inf), axis=-1).astype(v.dtype)
        return jnp.einsum('bhqk,bkhv->bqhv', p, v)

    o = lax.map(block, jnp.arange(seqlen // ATTN_BLOCK))
    o = jnp.moveaxis(o, 0, 1).reshape(bsz, seqlen, H * MLA_V)
    return o @ w_o


def swiglu(h, w_gate, w_up, w_down):
    return (jax.nn.silu(h @ w_gate) * (h @ w_up)) @ w_down


def setup_inputs(seed: int = 0) -> dict:
    key = jax.random.key(seed)
    keys = iter(jax.random.split(key, 40))

    def nrm(shape, scale):
        return jax.random.normal(next(keys), shape, jnp.float32) * scale

    def gain(shape):
        return 1.0 + nrm(shape, 0.02)

    NE, NO = N_EVEN, N_ODD
    x = nrm((BATCH, SEQ, D_MODEL), 1.0)
    u = jax.random.uniform(next(keys), (NE, SSD_HEADS), jnp.float32)
    dt0 = jnp.exp(u * (math.log(0.1) - math.log(0.001)) + math.log(0.001))
    dt_bias = dt0 + jnp.log(-jnp.expm1(-dt0))
    a_log = jnp.log(jax.random.uniform(next(keys), (NE, SSD_HEADS), jnp.float32, 1.0, 16.0))
    return {
        'x': x,
        'mix_norm_even': gain((NE, D_MODEL)),
        'w_in_even': nrm((NE, D_MODEL, IN_COLS), D_MODEL ** -0.5),
        'conv_w': nrm((NE, CONV_WIDTH, CONV_DIM), CONV_WIDTH ** -0.5),
        'conv_b': nrm((NE, CONV_DIM), 0.02),
        'dt_bias': dt_bias,
        'a_log': a_log,
        'd_skip': 1.0 + nrm((NE, SSD_HEADS), 0.1),
        'ssd_norm': gain((NE, D_SSM)),
        'gla_gk_w2': nrm((NE, GLA_GATE_RANK, GLA_KEY_DIM), GLA_GATE_RANK ** -0.5),
        'gla_gk_b': nrm((NE, GLA_KEY_DIM), 0.1),
        'gla_norm': gain((NE, GLA_HEAD_V)),
        'w_out_even': nrm((NE, MIX_WIDTH, D_MODEL), MIX_WIDTH ** -0.5),
        'mix_norm_odd': gain((NO, D_MODEL)),
        'w_dqkv': nrm((NO, D_MODEL, MLA_Q_RANK + MLA_KV_RANK + MLA_ROPE), D_MODEL ** -0.5),
        'q_lora_norm': gain((NO, MLA_Q_RANK)),
        'w_uq': nrm((NO, MLA_Q_RANK, MLA_HEADS * (MLA_NOPE + MLA_ROPE)), MLA_Q_RANK ** -0.5),
        'kv_lora_norm': gain((NO, MLA_KV_RANK)),
        'w_ukv': nrm((NO, MLA_KV_RANK, MLA_HEADS * (MLA_NOPE + MLA_V)), MLA_KV_RANK ** -0.5),
        'q_nope_norm': gain((NO, MLA_NOPE)),
        'q_rope_norm': gain((NO, MLA_ROPE)),
        'k_nope_norm': gain((NO, MLA_NOPE)),
        'k_rope_norm': gain((NO, MLA_ROPE)),
        'w_o_mla': nrm((NO, MLA_HEADS * MLA_V, D_MODEL), (MLA_HEADS * MLA_V) ** -0.5),
        'ffn_norm': gain((DEPTH, D_MODEL)),
        'w_gate': nrm((DEPTH, D_MODEL, FFN_HIDDEN), D_MODEL ** -0.5),
        'w_up': nrm((DEPTH, D_MODEL, FFN_HIDDEN), D_MODEL ** -0.5),
        'w_down': nrm((DEPTH, FFN_HIDDEN, D_MODEL), FFN_HIDDEN ** -0.5),
    }


def reference(x, mix_norm_even, w_in_even, conv_w, conv_b, dt_bias, a_log, d_skip, ssd_norm,
              gla_gk_w2, gla_gk_b, gla_norm, w_out_even,
              mix_norm_odd, w_dqkv, q_lora_norm, w_uq, kv_lora_norm, w_ukv,
              q_nope_norm, q_rope_norm, k_nope_norm, k_rope_norm, w_o_mla,
              ffn_norm, w_gate, w_up, w_down):
    for i in range(DEPTH):
        j = i // 2
        if i % 2 == 0:
            h = rms_norm(x, mix_norm_even[j])
            x = x + ssd_gla_layer(h, w_in_even[j], conv_w[j], conv_b[j], dt_bias[j], a_log[j],
                                  d_skip[j], ssd_norm[j], gla_gk_w2[j], gla_gk_b[j], gla_norm[j],
                                  w_out_even[j])
        else:
            h = rms_norm(x, mix_norm_odd[j])
            x = x + mla_layer(h, w_dqkv[j], q_lora_norm[j], w_uq[j], kv_lora_norm[j], w_ukv[j],
                              q_nope_norm[j], q_rope_norm[j], k_nope_norm[j], k_rope_norm[j],
                              w_o_mla[j])
        x = x + swiglu(rms_norm(x, ffn_norm[i]), w_gate[i], w_up[i], w_down[i])
    return x
```

```python
import functools
import math

import jax
import jax.numpy as jnp
from jax import lax
from jax.experimental import pallas as pl
from jax.experimental.pallas import tpu as pltpu

F32 = jnp.float32
BF16 = jnp.bfloat16

D_MODEL = 1024
NORM_EPS = 1e-6

SSD_HEAD_DIM = 64
SSD_HEADS = 16
SSD_GROUPS = 2
SSD_HPG = SSD_HEADS // SSD_GROUPS
SSD_STATE = 128
SSD_CHUNK = 128
CONV_WIDTH = 4
D_SSM = SSD_HEADS * SSD_HEAD_DIM
CONV_DIM = D_SSM + 2 * SSD_GROUPS * SSD_STATE

GLA_HEADS = 4
GLA_KEY_DIM = D_MODEL // 2
GLA_VAL_DIM = D_MODEL
GLA_HEAD_K = GLA_KEY_DIM // GLA_HEADS
GLA_HEAD_V = GLA_VAL_DIM // GLA_HEADS
GLA_GATE_RANK = 16
GLA_GATE_NORM = 16.0
GLA_CHUNK = 64

MLA_HEADS = 16
MLA_NOPE = 64
MLA_ROPE = 32
MLA_V = 64
MLA_Q_RANK = 384
MLA_KV_RANK = 256
ROPE_THETA = 10000.0

LANES = 128
HEAD_PAD = 128
V7X_VMEM_BYTES = 64 * 1024 * 1024
VMEM_LIMIT = V7X_VMEM_BYTES - 8 * 1024 * 1024


def _cparams(*sem):
    return pltpu.CompilerParams(dimension_semantics=sem, vmem_limit_bytes=VMEM_LIMIT)


def _dot(a, b):
    return jnp.dot(a, b, preferred_element_type=F32)


def _dot_nt(a, b):
    return lax.dot_general(a, b, (((1,), (1,)), ((), ())), preferred_element_type=F32)


def _dot_tn(a, b):
    return lax.dot_general(a, b, (((0,), (0,)), ((), ())), preferred_element_type=F32)


def _split3(x):
    hi = x.astype(BF16)
    r1 = x - hi.astype(F32)
    mid = r1.astype(BF16)
    lo = (r1 - mid.astype(F32)).astype(BF16)
    return hi, mid, lo


def _dot_sel(x, m):
    hi, mid, lo = _split3(x)
    return _dot(hi, m) + _dot(mid, m) + _dot(lo, m)


def _sel_dot(m, x):
    hi, mid, lo = _split3(x)
    return _dot(m, hi) + _dot(m, mid) + _dot(m, lo)


def _rms(x, w):
    return x * lax.rsqrt(jnp.mean(x * x, axis=-1, keepdims=True) + NORM_EPS) * w


def _silu(x):
    return x * jax.nn.sigmoid(x)


def _softplus(x):
    return jnp.maximum(x, 0.0) + jnp.log1p(jnp.exp(-jnp.abs(x)))


def _col_chunks(n, step):
    return [(c, min(c + step, n)) for c in range(0, n, step)]


def _const_spec(shape):
    return pl.BlockSpec(shape, lambda *_: (0,) * len(shape))


def _norm_proj_kernel(*refs, n_out, chunk):
    x_ref, nw_ref = refs[0], refs[1]
    w_refs = refs[2:2 + n_out]
    o_refs = refs[2 + n_out:]
    h = _rms(x_ref[...].astype(F32), nw_ref[...]).astype(BF16)
    for w_ref, o_ref in zip(w_refs, o_refs):
        for c0, c1 in _col_chunks(w_ref.shape[1], chunk):
            o_ref[:, c0:c1] = _dot(h, w_ref[:, c0:c1]).astype(o_ref.dtype)


def norm_proj(x, norm_w, weights, out_dtypes, *, tm=512, chunk=512):
    t, k = x.shape
    n_out = len(weights)
    in_specs = [pl.BlockSpec((tm, k), lambda i: (i, 0)), _const_spec((1, k))]
    in_specs += [_const_spec(w.shape) for w in weights]
    out_specs = [pl.BlockSpec((tm, w.shape[1]), lambda i: (i, 0)) for w in weights]
    out_shape = [jax.ShapeDtypeStruct((t, w.shape[1]), dt) for w, dt in zip(weights, out_dtypes)]
    return pl.pallas_call(
        functools.partial(_norm_proj_kernel, n_out=n_out, chunk=chunk),
        grid=(t // tm,),
        in_specs=in_specs,
        out_specs=out_specs,
        out_shape=out_shape,
        compiler_params=_cparams("parallel"),
        name="norm_proj",
    )(x, norm_w.reshape(1, k), *weights)


def _proj_res_kernel(*refs, n_in):
    x_ref = refs[0]
    a_refs = refs[1:1 + n_in]
    w_refs = refs[1 + n_in:1 + 2 * n_in]
    o_ref = refs[1 + 2 * n_in]
    acc = x_ref[...]
    for a_ref, w_ref in zip(a_refs, w_refs):
        acc = acc + _dot(a_ref[...], w_ref[...])
    o_ref[...] = acc


def proj_residual(x, acts, weights, *, tm=512):
    t, d = x.shape
    n_in = len(acts)
    in_specs = [pl.BlockSpec((tm, d), lambda i: (i, 0))]
    in_specs += [pl.BlockSpec((tm, a.shape[1]), lambda i: (i, 0)) for a in acts]
    in_specs += [_const_spec(w.shape) for w in weights]
    return pl.pallas_call(
        functools.partial(_proj_res_kernel, n_in=n_in),
        grid=(t // tm,),
        in_specs=in_specs,
        out_specs=pl.BlockSpec((tm, d), lambda i: (i, 0)),
        out_shape=jax.ShapeDtypeStruct((t, d), F32),
        compiler_params=_cparams("parallel"),
        name="proj_residual",
    )(x, *acts, *weights)


def _ffn_kernel(x_ref, nw_ref, wg_ref, wu_ref, wd_ref, o_ref, *, chunk):
    x = x_ref[...]
    h = _rms(x, nw_ref[...]).astype(BF16)
    acc = x
    for c0, c1 in _col_chunks(wg_ref.shape[1], chunk):
        g = _dot(h, wg_ref[:, c0:c1])
        u = _dot(h, wu_ref[:, c0:c1])
        a = (_silu(g) * u).astype(BF16)
        acc = acc + _dot(a, wd_ref[c0:c1, :])
    o_ref[...] = acc


def ffn_residual(x, norm_w, wg, wu, wd, *, tm=512, chunk=256):
    t, d = x.shape
    row = pl.BlockSpec((tm, d), lambda i: (i, 0))
    return pl.pallas_call(
        functools.partial(_ffn_kernel, chunk=chunk),
        grid=(t // tm,),
        in_specs=[row, _const_spec((1, d)), _const_spec(wg.shape), _const_spec(wu.shape), _const_spec(wd.shape)],
        out_specs=row,
        out_shape=jax.ShapeDtypeStruct((t, d), F32),
        compiler_params=_cparams("parallel"),
        name="ffn_residual",
    )(x, norm_w.reshape(1, d), wg, wu, wd)


_COL_Z = 0
_COL_XBC = D_SSM
_COL_Q = _COL_XBC + CONV_DIM
_COL_K = _COL_Q + GLA_KEY_DIM
_COL_V = _COL_K + GLA_KEY_DIM
_COL_G = _COL_V + GLA_VAL_DIM
_COL_GKL = _COL_G + GLA_VAL_DIM
_PROJ_COLS = _COL_GKL + LANES
_CONV_HALO = 8


def _ssd_kernel(z_ref, xs_ref, bc_ref, dt_ref, cw_ref, cb_ref, dtb_ref, nega_ref, dskip_ref, nw_ref,
                tril_ref, o_ref, xpad_ref, state_ref):
    q = SSD_CHUNK
    p = SSD_HEAD_DIM
    n = SSD_STATE
    gw = SSD_HPG * p

    @pl.when(pl.program_id(1) == 0)
    def _():
        xpad_ref[0:_CONV_HALO, :] = jnp.zeros((_CONV_HALO, CONV_DIM), F32)
        state_ref[...] = jnp.zeros_like(state_ref)

    xpad_ref[_CONV_HALO:, 0:D_SSM] = xs_ref[...].astype(F32)
    xpad_ref[_CONV_HALO:, D_SSM:] = bc_ref[...].astype(F32)
    conv = cb_ref[...]
    for j in range(CONV_WIDTH):
        off = _CONV_HALO - (CONV_WIDTH - 1) + j
        conv = conv + cw_ref[j:j + 1, :] * xpad_ref[off:off + q, :]
    xpad_ref[0:_CONV_HALO, :] = xpad_ref[q:q + _CONV_HALO, :]
    xbc = _silu(conv)

    dt = _softplus(dt_ref[...] + dtb_ref[...])
    a = dt * nega_ref[...]
    acum = _sel_dot(tril_ref[...], a)
    acum_t = acum.T
    row = lax.broadcasted_iota(jnp.int32, (q, q), 0)
    col = lax.broadcasted_iota(jnp.int32, (q, q), 1)
    causal = row >= col

    y_groups = []
    for g in range(SSD_GROUPS):
        b_g = xbc[:, D_SSM + g * n:D_SSM + (g + 1) * n]
        c_g = xbc[:, D_SSM + (SSD_GROUPS + g) * n:D_SSM + (SSD_GROUPS + g + 1) * n]
        b_bf = b_g.astype(BF16)
        c_bf = c_g.astype(BF16)
        cb = _dot_nt(c_bf, b_bf)
        s_g = state_ref[g]
        y_off = _dot(c_bf, s_g.astype(BF16))
        y_heads, w_heads, cd_heads = [], [], []
        for k in range(SSD_HPG):
            hk = g * SSD_HPG + k
            a_col = acum[:, hk:hk + 1]
            a_row = acum_t[hk:hk + 1, :]
            a_last = acum[q - 1:q, hk:hk + 1]
            decay = jnp.where(causal, jnp.exp(jnp.minimum(a_col - a_row, 0.0)), 0.0)
            x_k = xbc[:, hk * p:(hk + 1) * p]
            xdt = x_k * dt[:, hk:hk + 1]
            y_diag = _dot((cb * decay).astype(BF16), xdt.astype(BF16))
            y_k = y_diag + y_off[:, k * p:(k + 1) * p] * jnp.exp(a_col) + x_k * dskip_ref[:, hk:hk + 1]
            y_heads.append(y_k)
            w_heads.append((xdt * jnp.exp(a_last - a_col)).astype(BF16))
            cd_heads.append(jnp.broadcast_to(jnp.exp(a_last), (1, p)))
        w_g = jnp.concatenate(w_heads, axis=1)
        cd_g = jnp.concatenate(cd_heads, axis=1)
        state_ref[g] = s_g * cd_g + _dot_tn(b_bf, w_g)
        y_g = jnp.concatenate(y_heads, axis=1)
        y_g = y_g * _silu(z_ref[:, g * gw:(g + 1) * gw].astype(F32))
        y_groups.append(_rms(y_g, nw_ref[:, g * gw:(g + 1) * gw]))
    o_ref[...] = jnp.concatenate(y_groups, axis=1).astype(o_ref.dtype)


def ssd_scan(proj, dt_raw, conv_w, conv_b, dt_bias, a_log, d_skip, norm_w, *, batch):
    t = proj.shape[0]
    q = SSD_CHUNK
    nc = t // batch // q

    def pad_heads(v):
        return jnp.pad(v.reshape(1, SSD_HEADS), ((0, 0), (0, LANES - SSD_HEADS)))

    tril = jnp.tril(jnp.ones((q, q), F32)).astype(BF16)
    rows = lambda b, c: b * nc + c
    return pl.pallas_call(
        _ssd_kernel,
        grid=(batch, nc),
        in_specs=[
            pl.BlockSpec((q, D_SSM), lambda b, c: (rows(b, c), _COL_Z // D_SSM)),
            pl.BlockSpec((q, D_SSM), lambda b, c: (rows(b, c), _COL_XBC // D_SSM)),
            pl.BlockSpec((q, CONV_DIM - D_SSM), lambda b, c: (rows(b, c), (_COL_XBC + D_SSM) // (CONV_DIM - D_SSM))),
            pl.BlockSpec((q, LANES), lambda b, c: (rows(b, c), 0)),
            _const_spec((CONV_WIDTH, CONV_DIM)),
            _const_spec((1, CONV_DIM)),
            _const_spec((1, LANES)),
            _const_spec((1, LANES)),
            _const_spec((1, LANES)),
            _const_spec((1, D_SSM)),
            _const_spec((q, q)),
        ],
        out_specs=pl.BlockSpec((q, D_SSM), lambda b, c: (rows(b, c), 0)),
        out_shape=jax.ShapeDtypeStruct((t, D_SSM), BF16),
        scratch_shapes=[
            pltpu.VMEM((q + _CONV_HALO, CONV_DIM), F32),
            pltpu.VMEM((SSD_GROUPS, SSD_STATE, SSD_HPG * SSD_HEAD_DIM), F32),
        ],
        compiler_params=_cparams("parallel", "arbitrary"),
        name="ssd_scan",
    )(proj, proj, proj, dt_raw, conv_w, conv_b.reshape(1, CONV_DIM), pad_heads(dt_bias),
      pad_heads(-jnp.exp(a_log)), pad_heads(d_skip), norm_w.reshape(1, D_SSM), tril)


def _gla_kernel(q_ref, k_ref, v_ref, g_ref, gkl_ref, w2_ref, gkb_ref, nw_ref, tril_ref, o_ref, state_ref,
                *, block):
    c = GLA_CHUNK
    scale = GLA_HEAD_K ** -0.5

    @pl.when(pl.program_id(2) == 0)
    def _():
        state_ref[...] = jnp.zeros_like(state_ref)

    gk = -_softplus(-(_dot(gkl_ref[...], w2_ref[...]) + gkb_ref[...])) * (1.0 / GLA_GATE_NORM)
    row = lax.broadcasted_iota(jnp.int32, (c, c), 0)
    col = lax.broadcasted_iota(jnp.int32, (c, c), 1)
    causal = row >= col
    for j in range(block // c):
        r0, r1 = j * c, (j + 1) * c
        gcum = _sel_dot(tril_ref[...], gk[r0:r1])
        g_last = gcum[c - 1:c, :]
        qc = q_ref[r0:r1, :].astype(F32) * scale
        kc = k_ref[r0:r1, :].astype(F32)
        vc = v_ref[r0:r1, :]
        q_dec = (qc * jnp.exp(gcum)).astype(BF16)
        k_inv = (kc * jnp.exp(-gcum)).astype(BF16)
        k_end = (kc * jnp.exp(g_last - gcum)).astype(BF16)
        scores = jnp.where(causal, _dot_nt(q_dec, k_inv), 0.0)
        s_t = state_ref[...]
        o = _dot(scores.astype(BF16), vc) + _dot_nt(q_dec, s_t.astype(BF16))
        state_ref[...] = s_t * jnp.exp(g_last) + _dot_tn(vc, k_end)
        o = _rms(o, nw_ref[...]) * _silu(g_ref[r0:r1, :].astype(F32))
        o_ref[r0:r1, :] = o.astype(o_ref.dtype)


def gla_scan(proj, gk_w2, gk_b, norm_w, *, batch, block=256):
    t = proj.shape[0]
    nb = t // batch // block
    dk, dv = GLA_HEAD_K, GLA_HEAD_V
    w2 = jnp.pad(gk_w2, ((0, LANES - GLA_GATE_RANK), (0, 0))).astype(BF16)
    tril = jnp.tril(jnp.ones((GLA_CHUNK, GLA_CHUNK), F32)).astype(BF16)
    rows = lambda b, i: b * nb + i
    return pl.pallas_call(
        functools.partial(_gla_kernel, block=block),
        grid=(batch, GLA_HEADS, nb),
        in_specs=[
            pl.BlockSpec((block, dk), lambda b, h, i: (rows(b, i), _COL_Q // dk + h)),
            pl.BlockSpec((block, dk), lambda b, h, i: (rows(b, i), _COL_K // dk + h)),
            pl.BlockSpec((block, dv), lambda b, h, i: (rows(b, i), _COL_V // dv + h)),
            pl.BlockSpec((block, dv), lambda b, h, i: (rows(b, i), _COL_G // dv + h)),
            pl.BlockSpec((block, LANES), lambda b, h, i: (rows(b, i), _COL_GKL // LANES)),
            pl.BlockSpec((LANES, dk), lambda b, h, i: (0, h)),
            pl.BlockSpec((1, dk), lambda b, h, i: (0, h)),
            _const_spec((1, dv)),
            _const_spec((GLA_CHUNK, GLA_CHUNK)),
        ],
        out_specs=pl.BlockSpec((block, dv), lambda b, h, i: (rows(b, i), h)),
        out_shape=jax.ShapeDtypeStruct((t, GLA_VAL_DIM), BF16),
        scratch_shapes=[pltpu.VMEM((dv, dk), F32)],
        compiler_params=_cparams("parallel", "parallel", "arbitrary"),
        name="gla_scan",
    )(proj, proj, proj, proj, proj, w2, gk_b.reshape(1, GLA_KEY_DIM), norm_w.reshape(1, dv), tril)


_DQ_CQ = 0
_DQ_CKV = 512
_DQ_KRA = 768
_DQ_KRB = 896
_DQ_COLS = 1024
_QKV_COLS = MLA_HEADS * HEAD_PAD


def _q_up_kernel(cq_ref, lw_ref, wa_ref, wb_ref, ga_ref, gb_ref, cos_ref, sin_ref, ind_ref, indt_ref, o_ref,
                 *, scale):
    c = _rms(cq_ref[...], lw_ref[...]).astype(BF16)
    cos = cos_ref[...]
    sin = sin_ref[...]
    ga = ga_ref[...] * scale
    gb = gb_ref[...] * scale
    for h in range(MLA_HEADS):
        sl = slice(h * HEAD_PAD, (h + 1) * HEAD_PAD)
        a = _dot(c, wa_ref[:, sl])
        b = _dot(c, wb_ref[:, sl])
        ms = _dot_sel(a * a, ind_ref[...])
        r = _dot_sel(lax.rsqrt(ms + NORM_EPS), indt_ref[...])
        o_ref[:, sl] = ((a * (ga * cos) + b * (gb * sin)) * r).astype(o_ref.dtype)


def _kv_up_kernel(ckv_ref, kra_ref, krb_ref, lw_ref, wk_ref, wv_ref, gk_ref, gra_ref, grb_ref, cos_ref, sin_ref,
                  ind_ref, indt_ref, ones_ref, k_ref, v_ref):
    c = _rms(ckv_ref[...], lw_ref[...]).astype(BF16)
    kra = kra_ref[...]
    krb = krb_ref[...]
    r_rope = lax.rsqrt(jnp.sum(kra * kra, axis=-1, keepdims=True) * (1.0 / MLA_ROPE) + NORM_EPS)
    k_rope = (kra * (gra_ref[...] * cos_ref[...]) + krb * (grb_ref[...] * sin_ref[...])) * r_rope
    gk = gk_ref[...]
    ones = ones_ref[...]
    for h in range(MLA_HEADS):
        sl = slice(h * HEAD_PAD, (h + 1) * HEAD_PAD)
        kn = _dot(c, wk_ref[:, sl])
        ms = _dot_sel(kn * kn, ind_ref[...])
        r = _dot_sel(lax.rsqrt(ms + NORM_EPS), indt_ref[...])
        k_ref[:, sl] = (kn * gk * r + k_rope).astype(k_ref.dtype)
        v_ref[:, sl] = (_dot(c, wv_ref[:, sl]) + ones).astype(v_ref.dtype)


def _head_selectors():
    lane = jnp.arange(HEAD_PAD)
    nope = (lane < MLA_NOPE).astype(F32)
    rope = ((lane >= MLA_NOPE) & (lane < MLA_NOPE + MLA_ROPE)).astype(F32)
    ind = jnp.zeros((HEAD_PAD, LANES), F32)
    ind = ind.at[:, 0].set(nope / MLA_NOPE).at[:, 1].set(rope / MLA_ROPE)
    indt = jnp.zeros((LANES, HEAD_PAD), F32).at[0].set(nope).at[1].set(rope)
    return ind.astype(BF16), indt.astype(BF16)


def _rope_tables(seqlen, batch):
    pos = jnp.arange(seqlen, dtype=F32)
    inv = 1.0 / (ROPE_THETA ** (jnp.arange(0, MLA_ROPE, 2, dtype=F32) / MLA_ROPE))
    ang = pos[:, None] * inv[None, :]
    cos, sin = jnp.cos(ang), jnp.sin(ang)
    pad = jnp.zeros((seqlen, HEAD_PAD - MLA_NOPE - MLA_ROPE), F32)
    cos_t = jnp.concatenate([jnp.ones((seqlen, MLA_NOPE), F32), cos, cos, pad], axis=1)
    sin_t = jnp.concatenate([jnp.zeros((seqlen, MLA_NOPE), F32), -sin, sin, pad], axis=1)
    return jnp.tile(cos_t, (batch, 1)), jnp.tile(sin_t, (batch, 1))


def _swap_halves(w):
    half = w.shape[-1] // 2
    return jnp.concatenate([w[..., half:], w[..., :half]], axis=-1)


def _rope_lanes(w):
    pad = [(0, 0)] * (w.ndim - 1) + [(MLA_NOPE, HEAD_PAD - MLA_NOPE - MLA_ROPE)]
    return jnp.pad(w, pad)


def q_up(cqkv, lora_norm, w_uq, nope_norm, rope_norm, cos_t, sin_t, *, tm=512):
    t = cqkv.shape[0]
    w = w_uq.reshape(MLA_Q_RANK, MLA_HEADS, MLA_NOPE + MLA_ROPE)
    w_nope, w_rope = w[..., :MLA_NOPE], w[..., MLA_NOPE:]
    pad = jnp.zeros((MLA_Q_RANK, MLA_HEADS, HEAD_PAD - MLA_NOPE - MLA_ROPE), F32)
    wa = jnp.concatenate([w_nope, w_rope, pad], axis=-1).reshape(MLA_Q_RANK, _QKV_COLS).astype(BF16)
    wb = _rope_lanes(_swap_halves(w_rope)).reshape(MLA_Q_RANK, _QKV_COLS).astype(BF16)
    ga = jnp.concatenate([nope_norm, rope_norm, jnp.zeros((HEAD_PAD - MLA_NOPE - MLA_ROPE,), F32)]).reshape(1, HEAD_PAD)
    gb = _rope_lanes(_swap_halves(rope_norm)).reshape(1, HEAD_PAD)
    ind, indt = _head_selectors()
    row = lambda width, j: pl.BlockSpec((tm, width), lambda i: (i, j))
    return pl.pallas_call(
        functools.partial(_q_up_kernel, scale=(MLA_NOPE + MLA_ROPE) ** -0.5),
        grid=(t // tm,),
        in_specs=[
            row(MLA_Q_RANK, _DQ_CQ // MLA_Q_RANK),
            _const_spec((1, MLA_Q_RANK)),
            _const_spec(wa.shape), _const_spec(wb.shape),
            _const_spec((1, HEAD_PAD)), _const_spec((1, HEAD_PAD)),
            row(HEAD_PAD, 0), row(HEAD_PAD, 0),
            _const_spec(ind.shape), _const_spec(indt.shape),
        ],
        out_specs=row(_QKV_COLS, 0),
        out_shape=jax.ShapeDtypeStruct((t, _QKV_COLS), BF16),
        compiler_params=_cparams("parallel"),
        name="mla_q_up",
    )(cqkv, lora_norm.reshape(1, MLA_Q_RANK), wa, wb, ga, gb, cos_t, sin_t, ind, indt)


def kv_up(cqkv, lora_norm, w_ukv, nope_norm, rope_norm, cos_t, sin_t, *, tm=512):
    t = cqkv.shape[0]
    w = w_ukv.reshape(MLA_KV_RANK, MLA_HEADS, MLA_NOPE + MLA_V)
    pad_k = [(0, 0), (0, 0), (0, HEAD_PAD - MLA_NOPE)]
    pad_v = [(0, 0), (0, 0), (0, HEAD_PAD - MLA_V)]
    wk = jnp.pad(w[..., :MLA_NOPE], pad_k).reshape(MLA_KV_RANK, _QKV_COLS).astype(BF16)
    wv = jnp.pad(w[..., MLA_NOPE:], pad_v).reshape(MLA_KV_RANK, _QKV_COLS).astype(BF16)
    gk = jnp.pad(nope_norm, (0, HEAD_PAD - MLA_NOPE)).reshape(1, HEAD_PAD)
    gra = _rope_lanes(rope_norm).reshape(1, HEAD_PAD)
    grb = _rope_lanes(_swap_halves(rope_norm)).reshape(1, HEAD_PAD)
    ones = (jnp.arange(HEAD_PAD) >= MLA_V).astype(F32).reshape(1, HEAD_PAD)
    ind, indt = _head_selectors()
    row = lambda width, j: pl.BlockSpec((tm, width), lambda i: (i, j))
    return pl.pallas_call(
        _kv_up_kernel,
        grid=(t // tm,),
        in_specs=[
            row(MLA_KV_RANK, _DQ_CKV // MLA_KV_RANK),
            row(HEAD_PAD, _DQ_KRA // HEAD_PAD), row(HEAD_PAD, _DQ_KRB // HEAD_PAD),
            _const_spec((1, MLA_KV_RANK)),
            _const_spec(wk.shape), _const_spec(wv.shape),
            _const_spec((1, HEAD_PAD)), _const_spec((1, HEAD_PAD)), _const_spec((1, HEAD_PAD)),
            row(HEAD_PAD, 0), row(HEAD_PAD, 0),
            _const_spec(ind.shape), _const_spec(indt.shape), _const_spec((1, HEAD_PAD)),
        ],
        out_specs=[row(_QKV_COLS, 0), row(_QKV_COLS, 0)],
        out_shape=[jax.ShapeDtypeStruct((t, _QKV_COLS), BF16)] * 2,
        compiler_params=_cparams("parallel"),
        name="mla_kv_up",
    )(cqkv, cqkv, cqkv, lora_norm.reshape(1, MLA_KV_RANK), wk, wv, gk, gra, grb, cos_t, sin_t, ind, indt, ones)


def _attn_kernel(q_ref, k_ref, v_ref, o_ref, *, tq):
    qi = pl.program_id(2)
    q = q_ref[...]

    def step(kb, vb, m, acc, mask):
        s = _dot_nt(q, kb)
        if mask is not None:
            s = jnp.where(mask, s, -jnp.inf)
        m_new = jnp.maximum(m, jnp.max(s, axis=-1, keepdims=True))
        p = jnp.exp(s - m_new)
        acc = jnp.exp(m - m_new) * acc + _dot(p.astype(BF16), vb)
        return m_new, acc

    def body(j, carry):
        off = pl.multiple_of(j * tq, tq)
        return step(k_ref[pl.ds(off, tq), :], v_ref[pl.ds(off, tq), :], *carry, None)

    m0 = jnp.full((tq, 1), -jnp.inf, F32)
    acc0 = jnp.zeros((tq, HEAD_PAD), F32)
    m, acc = lax.fori_loop(0, qi, body, (m0, acc0))
    off = pl.multiple_of(qi * tq, tq)
    row = lax.broadcasted_iota(jnp.int32, (tq, tq), 0)
    col = lax.broadcasted_iota(jnp.int32, (tq, tq), 1)
    m, acc = step(k_ref[pl.ds(off, tq), :], v_ref[pl.ds(off, tq), :], m, acc, row >= col)
    o_ref[...] = (acc / acc[:, MLA_V:MLA_V + 1]).astype(o_ref.dtype)


def flash_attention(q, k, v, *, batch, tq=512):
    t = q.shape[0]
    seqlen = t // batch
    nq = seqlen // tq
    return pl.pallas_call(
        functools.partial(_attn_kernel, tq=tq),
        grid=(batch, MLA_HEADS, nq),
        in_specs=[
            pl.BlockSpec((tq, HEAD_PAD), lambda b, h, i: (b * nq + i, h)),
            pl.BlockSpec((seqlen, HEAD_PAD), lambda b, h, i: (b, h)),
            pl.BlockSpec((seqlen, HEAD_PAD), lambda b, h, i: (b, h)),
        ],
        out_specs=pl.BlockSpec((tq, HEAD_PAD), lambda b, h, i: (b * nq + i, h)),
        out_shape=jax.ShapeDtypeStruct((t, _QKV_COLS), BF16),
        compiler_params=_cparams("parallel", "parallel", "arbitrary"),
        name="mla_attention",
    )(q, k, v)


def _even_layer(x, batch, norm_w, w_in, conv_w, conv_b, dt_bias, a_log, d_skip, ssd_norm,
                gk_w2, gk_b, gla_norm, w_out):
    sizes = (D_SSM, CONV_DIM, SSD_HEADS, GLA_KEY_DIM, GLA_KEY_DIM, GLA_VAL_DIM, GLA_VAL_DIM, GLA_GATE_RANK)
    offs = [0]
    for s in sizes:
        offs.append(offs[-1] + s)
    seg = [w_in[:, offs[i]:offs[i + 1]] for i in range(len(sizes))]
    w_z, w_xbc, w_dt, w_q, w_k, w_v, w_g, w_gkl = seg
    w_main = jnp.concatenate(
        [w_z, w_xbc, w_q, w_k, w_v, w_g, jnp.pad(w_gkl, ((0, 0), (0, LANES - GLA_GATE_RANK)))], axis=1).astype(BF16)
    w_dtp = jnp.pad(w_dt, ((0, 0), (0, LANES - SSD_HEADS))).astype(BF16)
    proj, dt_raw = norm_proj(x, norm_w, [w_main, w_dtp], [BF16, F32])
    y_ssd = ssd_scan(proj, dt_raw, conv_w, conv_b, dt_bias, a_log, d_skip, ssd_norm, batch=batch)
    y_gla = gla_scan(proj, gk_w2, gk_b, gla_norm, batch=batch)
    w_o = w_out.astype(BF16)
    return proj_residual(x, [y_ssd, y_gla], [w_o[:D_SSM], w_o[D_SSM:]])


def _odd_layer(x, batch, cos_t, sin_t, norm_w, w_dqkv, q_lora_norm, w_uq, kv_lora_norm, w_ukv,
               q_nope_norm, q_rope_norm, k_nope_norm, k_rope_norm, w_o):
    w_cq = w_dqkv[:, :MLA_Q_RANK]
    w_ckv = w_dqkv[:, MLA_Q_RANK:MLA_Q_RANK + MLA_KV_RANK]
    w_kr = w_dqkv[:, MLA_Q_RANK + MLA_KV_RANK:]
    w_d = jnp.concatenate(
        [w_cq, jnp.zeros((D_MODEL, _DQ_CKV - MLA_Q_RANK), F32), w_ckv, _rope_lanes(w_kr),
         _rope_lanes(_swap_halves(w_kr))], axis=1).astype(BF16)
    (cqkv,) = norm_proj(x, norm_w, [w_d], [F32])
    q = q_up(cqkv, q_lora_norm, w_uq, q_nope_norm, q_rope_norm, cos_t, sin_t)
    k, v = kv_up(cqkv, kv_lora_norm, w_ukv, k_nope_norm, k_rope_norm, cos_t, sin_t)
    o = flash_attention(q, k, v, batch=batch)
    w_oe = jnp.pad(w_o.reshape(MLA_HEADS, MLA_V, D_MODEL), ((0, 0), (0, HEAD_PAD - MLA_V), (0, 0)))
    return proj_residual(x, [o], [w_oe.reshape(_QKV_COLS, D_MODEL).astype(BF16)])


def kernel(x, mix_norm_even, w_in_even, conv_w, conv_b, dt_bias, a_log, d_skip, ssd_norm, gla_gk_w2, gla_gk_b, gla_norm, w_out_even, mix_norm_odd, w_dqkv, q_lora_norm, w_uq, kv_lora_norm, w_ukv, q_nope_norm, q_rope_norm, k_nope_norm, k_rope_norm, w_o_mla, ffn_norm, w_gate, w_up, w_down):
    batch, seqlen, d = x.shape
    depth = ffn_norm.shape[0]
    cos_t, sin_t = _rope_tables(seqlen, batch)
    x = x.reshape(batch * seqlen, d)
    for i in range(depth):
        j = i // 2
        if i % 2 == 0:
            x = _even_layer(x, batch, mix_norm_even[j], w_in_even[j], conv_w[j], conv_b[j], dt_bias[j], a_log[j],
                            d_skip[j], ssd_norm[j], gla_gk_w2[j], gla_gk_b[j], gla_norm[j], w_out_even[j])
        else:
            x = _odd_layer(x, batch, cos_t, sin_t, mix_norm_odd[j], w_dqkv[j], q_lora_norm[j], w_uq[j],
                           kv_lora_norm[j], w_ukv[j], q_nope_norm[j], q_rope_norm[j], k_nope_norm[j],
                           k_rope_norm[j], w_o_mla[j])
        x = ffn_residual(x, ffn_norm[i], w_gate[i].astype(BF16), w_up[i].astype(BF16), w_down[i].astype(BF16))
    return x.reshape(batch, seqlen, d)
```

```python
import functools
import math

import jax
import jax.numpy as jnp
from jax import lax
from jax.experimental import pallas as pl
from jax.experimental.pallas import tpu as pltpu

F32 = jnp.float32
BF16 = jnp.bfloat16

D_MODEL = 1024
NORM_EPS = 1e-6

SSD_HEAD_DIM = 64
SSD_HEADS = 16
SSD_GROUPS = 2
SSD_HPG = SSD_HEADS // SSD_GROUPS
SSD_STATE = 128
SSD_CHUNK = 128
CONV_WIDTH = 4
D_SSM = SSD_HEADS * SSD_HEAD_DIM
CONV_DIM = D_SSM + 2 * SSD_GROUPS * SSD_STATE

GLA_HEADS = 4
GLA_KEY_DIM = D_MODEL // 2
GLA_VAL_DIM = D_MODEL
GLA_HEAD_K = GLA_KEY_DIM // GLA_HEADS
GLA_HEAD_V = GLA_VAL_DIM // GLA_HEADS
GLA_GATE_RANK = 16
GLA_GATE_NORM = 16.0
GLA_CHUNK = 64

MLA_HEADS = 16
MLA_NOPE = 64
MLA_ROPE = 32
MLA_V = 64
MLA_Q_RANK = 384
MLA_KV_RANK = 256
ROPE_THETA = 10000.0

LANES = 128
HEAD_PAD = 128
V7X_VMEM_BYTES = 64 * 1024 * 1024
VMEM_LIMIT = V7X_VMEM_BYTES - 8 * 1024 * 1024


def _cparams(*sem):
    return pltpu.CompilerParams(dimension_semantics=sem, vmem_limit_bytes=VMEM_LIMIT)


def _dot(a, b):
    return jnp.dot(a, b, preferred_element_type=F32)


def _dot_nt(a, b):
    return lax.dot_general(a, b, (((1,), (1,)), ((), ())), preferred_element_type=F32)


def _dot_tn(a, b):
    return lax.dot_general(a, b, (((0,), (0,)), ((), ())), preferred_element_type=F32)


def _split3(x):
    hi = x.astype(BF16)
    r1 = x - hi.astype(F32)
    mid = r1.astype(BF16)
    lo = (r1 - mid.astype(F32)).astype(BF16)
    return hi, mid, lo


def _dot_sel(x, m):
    hi, mid, lo = _split3(x)
    return _dot(hi, m) + _dot(mid, m) + _dot(lo, m)


def _sel_dot(m, x):
    hi, mid, lo = _split3(x)
    return _dot(m, hi) + _dot(m, mid) + _dot(m, lo)


def _rms(x, w):
    return x * lax.rsqrt(jnp.mean(x * x, axis=-1, keepdims=True) + NORM_EPS) * w


def _silu(x):
    return x * jax.nn.sigmoid(x)


def _softplus(x):
    return jnp.maximum(x, 0.0) + jnp.log1p(jnp.exp(-jnp.abs(x)))


def _col_chunks(n, step):
    return [(c, min(c + step, n)) for c in range(0, n, step)]


def _const_spec(shape):
    return pl.BlockSpec(shape, lambda *_: (0,) * len(shape))


def _norm_proj_kernel(*refs, n_out, chunk):
    x_ref, nw_ref = refs[0], refs[1]
    w_refs = refs[2:2 + n_out]
    o_refs = refs[2 + n_out:]
    h = _rms(x_ref[...].astype(F32), nw_ref[...]).astype(BF16)
    for w_ref, o_ref in zip(w_refs, o_refs):
        for c0, c1 in _col_chunks(w_ref.shape[1], chunk):
            o_ref[:, c0:c1] = _dot(h, w_ref[:, c0:c1]).astype(o_ref.dtype)


def norm_proj(x, norm_w, weights, out_dtypes, *, tm=512, chunk=512):
    t, k = x.shape
    n_out = len(weights)
    in_specs = [pl.BlockSpec((tm, k), lambda i: (i, 0)), _const_spec((1, k))]
    in_specs += [_const_spec(w.shape) for w in weights]
    out_specs = [pl.BlockSpec((tm, w.shape[1]), lambda i: (i, 0)) for w in weights]
    out_shape = [jax.ShapeDtypeStruct((t, w.shape[1]), dt) for w, dt in zip(weights, out_dtypes)]
    return pl.pallas_call(
        functools.partial(_norm_proj_kernel, n_out=n_out, chunk=chunk),
        grid=(t // tm,),
        in_specs=in_specs,
        out_specs=out_specs,
        out_shape=out_shape,
        compiler_params=_cparams("parallel"),
        name="norm_proj",
    )(x, norm_w.reshape(1, k), *weights)


def _proj_res_kernel(*refs, n_in):
    x_ref = refs[0]
    a_refs = refs[1:1 + n_in]
    w_refs = refs[1 + n_in:1 + 2 * n_in]
    o_ref = refs[1 + 2 * n_in]
    acc = x_ref[...]
    for a_ref, w_ref in zip(a_refs, w_refs):
        acc = acc + _dot(a_ref[...], w_ref[...])
    o_ref[...] = acc


def proj_residual(x, acts, weights, *, tm=512):
    t, d = x.shape
    n_in = len(acts)
    in_specs = [pl.BlockSpec((tm, d), lambda i: (i, 0))]
    in_specs += [pl.BlockSpec((tm, a.shape[1]), lambda i: (i, 0)) for a in acts]
    in_specs += [_const_spec(w.shape) for w in weights]
    return pl.pallas_call(
        functools.partial(_proj_res_kernel, n_in=n_in),
        grid=(t // tm,),
        in_specs=in_specs,
        out_specs=pl.BlockSpec((tm, d), lambda i: (i, 0)),
        out_shape=jax.ShapeDtypeStruct((t, d), F32),
        compiler_params=_cparams("parallel"),
        name="proj_residual",
    )(x, *acts, *weights)


def _ffn_kernel(x_ref, nw_ref, wg_ref, wu_ref, wd_ref, o_ref, *, chunk):
    x = x_ref[...]
    h = _rms(x, nw_ref[...]).astype(BF16)
    acc = x
    for c0, c1 in _col_chunks(wg_ref.shape[1], chunk):
        g = _dot(h, wg_ref[:, c0:c1])
        u = _dot(h, wu_ref[:, c0:c1])
        a = (_silu(g) * u).astype(BF16)
        acc = acc + _dot(a, wd_ref[c0:c1, :])
    o_ref[...] = acc


def ffn_residual(x, norm_w, wg, wu, wd, *, tm=512, chunk=256):
    t, d = x.shape
    row = pl.BlockSpec((tm, d), lambda i: (i, 0))
    return pl.pallas_call(
        functools.partial(_ffn_kernel, chunk=chunk),
        grid=(t // tm,),
        in_specs=[row, _const_spec((1, d)), _const_spec(wg.shape), _const_spec(wu.shape), _const_spec(wd.shape)],
        out_specs=row,
        out_shape=jax.ShapeDtypeStruct((t, d), F32),
        compiler_params=_cparams("parallel"),
        name="ffn_residual",
    )(x, norm_w.reshape(1, d), wg, wu, wd)


_COL_Z = 0
_COL_V = _COL_Z + D_SSM
_COL_G = _COL_V + GLA_VAL_DIM
_COL_XBC = _COL_G + GLA_VAL_DIM
_COL_Q = _COL_XBC + CONV_DIM
_COL_K = _COL_Q + GLA_KEY_DIM
_COL_GKL = _COL_K + GLA_KEY_DIM
_PROJ_COLS = _COL_GKL + LANES
_CONV_HALO = 8


def _ssd_kernel(z_ref, xs_ref, bc_ref, dt_ref, cw_ref, cb_ref, dtb_ref, nega_ref, dskip_ref, nw_ref,
                tril_ref, o_ref, xpad_ref, state_ref):
    q = SSD_CHUNK
    p = SSD_HEAD_DIM
    n = SSD_STATE
    gw = SSD_HPG * p

    @pl.when(pl.program_id(1) == 0)
    def _():
        xpad_ref[0:_CONV_HALO, :] = jnp.zeros((_CONV_HALO, CONV_DIM), F32)
        state_ref[...] = jnp.zeros_like(state_ref)

    xpad_ref[_CONV_HALO:, 0:D_SSM] = xs_ref[...].astype(F32)
    xpad_ref[_CONV_HALO:, D_SSM:] = bc_ref[...].astype(F32)
    conv = cb_ref[...]
    for j in range(CONV_WIDTH):
        off = _CONV_HALO - (CONV_WIDTH - 1) + j
        conv = conv + cw_ref[j:j + 1, :] * xpad_ref[off:off + q, :]
    xpad_ref[0:_CONV_HALO, :] = xpad_ref[q:q + _CONV_HALO, :]
    xbc = _silu(conv)

    dt = _softplus(dt_ref[...] + dtb_ref[...])
    a = dt * nega_ref[...]
    acum = _sel_dot(tril_ref[...], a)
    acum_t = acum.T
    row = lax.broadcasted_iota(jnp.int32, (q, q), 0)
    col = lax.broadcasted_iota(jnp.int32, (q, q), 1)
    causal = row >= col

    y_groups = []
    for g in range(SSD_GROUPS):
        b_g = xbc[:, D_SSM + g * n:D_SSM + (g + 1) * n]
        c_g = xbc[:, D_SSM + (SSD_GROUPS + g) * n:D_SSM + (SSD_GROUPS + g + 1) * n]
        b_bf = b_g.astype(BF16)
        c_bf = c_g.astype(BF16)
        cb = _dot_nt(c_bf, b_bf)
        s_g = state_ref[g]
        y_off = _dot(c_bf, s_g.astype(BF16))
        y_heads, w_heads, cd_heads = [], [], []
        for k in range(SSD_HPG):
            hk = g * SSD_HPG + k
            a_col = acum[:, hk:hk + 1]
            a_row = acum_t[hk:hk + 1, :]
            a_last = acum[q - 1:q, hk:hk + 1]
            decay = jnp.where(causal, jnp.exp(jnp.minimum(a_col - a_row, 0.0)), 0.0)
            x_k = xbc[:, hk * p:(hk + 1) * p]
            xdt = x_k * dt[:, hk:hk + 1]
            y_diag = _dot((cb * decay).astype(BF16), xdt.astype(BF16))
            y_k = y_diag + y_off[:, k * p:(k + 1) * p] * jnp.exp(a_col) + x_k * dskip_ref[:, hk:hk + 1]
            y_heads.append(y_k)
            w_heads.append((xdt * jnp.exp(a_last - a_col)).astype(BF16))
            cd_heads.append(jnp.broadcast_to(jnp.exp(a_last), (1, p)))
        w_g = jnp.concatenate(w_heads, axis=1)
        cd_g = jnp.concatenate(cd_heads, axis=1)
        state_ref[g] = s_g * cd_g + _dot_tn(b_bf, w_g)
        y_g = jnp.concatenate(y_heads, axis=1)
        y_g = y_g * _silu(z_ref[:, g * gw:(g + 1) * gw].astype(F32))
        y_groups.append(_rms(y_g, nw_ref[:, g * gw:(g + 1) * gw]))
    o_ref[...] = jnp.concatenate(y_groups, axis=1).astype(o_ref.dtype)


def ssd_scan(proj, dt_raw, conv_w, conv_b, dt_bias, a_log, d_skip, norm_w, *, batch):
    t = proj.shape[0]
    q = SSD_CHUNK
    nc = t // batch // q

    def pad_heads(v):
        return jnp.pad(v.reshape(1, SSD_HEADS), ((0, 0), (0, LANES - SSD_HEADS)))

    tril = jnp.tril(jnp.ones((q, q), F32)).astype(BF16)
    rows = lambda b, c: b * nc + c
    return pl.pallas_call(
        _ssd_kernel,
        grid=(batch, nc),
        in_specs=[
            pl.BlockSpec((q, D_SSM), lambda b, c: (rows(b, c), _COL_Z // D_SSM)),
            pl.BlockSpec((q, D_SSM), lambda b, c: (rows(b, c), _COL_XBC // D_SSM)),
            pl.BlockSpec((q, CONV_DIM - D_SSM), lambda b, c: (rows(b, c), (_COL_XBC + D_SSM) // (CONV_DIM - D_SSM))),
            pl.BlockSpec((q, LANES), lambda b, c: (rows(b, c), 0)),
            _const_spec((CONV_WIDTH, CONV_DIM)),
            _const_spec((1, CONV_DIM)),
            _const_spec((1, LANES)),
            _const_spec((1, LANES)),
            _const_spec((1, LANES)),
            _const_spec((1, D_SSM)),
            _const_spec((q, q)),
        ],
        out_specs=pl.BlockSpec((q, D_SSM), lambda b, c: (rows(b, c), 0)),
        out_shape=jax.ShapeDtypeStruct((t, D_SSM), BF16),
        scratch_shapes=[
            pltpu.VMEM((q + _CONV_HALO, CONV_DIM), F32),
            pltpu.VMEM((SSD_GROUPS, SSD_STATE, SSD_HPG * SSD_HEAD_DIM), F32),
        ],
        compiler_params=_cparams("parallel", "arbitrary"),
        name="ssd_scan",
    )(proj, proj, proj, dt_raw, conv_w, conv_b.reshape(1, CONV_DIM), pad_heads(dt_bias),
      pad_heads(-jnp.exp(a_log)), pad_heads(d_skip), norm_w.reshape(1, D_SSM), tril)


def _gla_kernel(q_ref, k_ref, v_ref, g_ref, gkl_ref, w2_ref, gkb_ref, nw_ref, tril_ref, o_ref, state_ref,
                *, block):
    c = GLA_CHUNK
    dk, dv = GLA_HEAD_K, GLA_HEAD_V
    scale = dk ** -0.5

    @pl.when(pl.program_id(1) == 0)
    def _():
        state_ref[...] = jnp.zeros_like(state_ref)

    gk = -_softplus(-(_dot(gkl_ref[...], w2_ref[...]) + gkb_ref[...])) * (1.0 / GLA_GATE_NORM)
    row = lax.broadcasted_iota(jnp.int32, (c, c), 0)
    col = lax.broadcasted_iota(jnp.int32, (c, c), 1)
    causal = row >= col
    for j in range(block // c):
        r0, r1 = j * c, (j + 1) * c
        gcum_all = _sel_dot(tril_ref[...], gk[r0:r1])
        for h in range(GLA_HEADS):
            ks, vs = slice(h * dk, (h + 1) * dk), slice(h * dv, (h + 1) * dv)
            gcum = gcum_all[:, ks]
            g_last = gcum[c - 1:c, :]
            qc = q_ref[r0:r1, ks].astype(F32) * scale
            kc = k_ref[r0:r1, ks].astype(F32)
            vc = v_ref[r0:r1, vs]
            q_dec = (qc * jnp.exp(gcum)).astype(BF16)
            k_inv = (kc * jnp.exp(-gcum)).astype(BF16)
            k_end = (kc * jnp.exp(g_last - gcum)).astype(BF16)
            scores = jnp.where(causal, _dot_nt(q_dec, k_inv), 0.0)
            s_t = state_ref[h]
            o = _dot(scores.astype(BF16), vc) + _dot_nt(q_dec, s_t.astype(BF16))
            state_ref[h] = s_t * jnp.exp(g_last) + _dot_tn(vc, k_end)
            o = _rms(o, nw_ref[...]) * _silu(g_ref[r0:r1, vs].astype(F32))
            o_ref[r0:r1, vs] = o.astype(o_ref.dtype)


def gla_scan(proj, gk_w2, gk_b, norm_w, *, batch, block=256):
    t = proj.shape[0]
    nb = t // batch // block
    kd, vd = GLA_KEY_DIM, GLA_VAL_DIM
    w2 = jnp.pad(gk_w2, ((0, LANES - GLA_GATE_RANK), (0, 0))).astype(BF16)
    tril = jnp.tril(jnp.ones((GLA_CHUNK, GLA_CHUNK), F32)).astype(BF16)
    rows = lambda b, i: b * nb + i
    return pl.pallas_call(
        functools.partial(_gla_kernel, block=block),
        grid=(batch, nb),
        in_specs=[
            pl.BlockSpec((block, kd), lambda b, i: (rows(b, i), _COL_Q // kd)),
            pl.BlockSpec((block, kd), lambda b, i: (rows(b, i), _COL_K // kd)),
            pl.BlockSpec((block, vd), lambda b, i: (rows(b, i), _COL_V // vd)),
            pl.BlockSpec((block, vd), lambda b, i: (rows(b, i), _COL_G // vd)),
            pl.BlockSpec((block, LANES), lambda b, i: (rows(b, i), _COL_GKL // LANES)),
            _const_spec((LANES, kd)),
            _const_spec((1, kd)),
            _const_spec((1, GLA_HEAD_V)),
            _const_spec((GLA_CHUNK, GLA_CHUNK)),
        ],
        out_specs=pl.BlockSpec((block, vd), lambda b, i: (rows(b, i), 0)),
        out_shape=jax.ShapeDtypeStruct((t, vd), BF16),
        scratch_shapes=[pltpu.VMEM((GLA_HEADS, GLA_HEAD_V, GLA_HEAD_K), F32)],
        compiler_params=_cparams("parallel", "arbitrary"),
        name="gla_scan",
    )(proj, proj, proj, proj, proj, w2, gk_b.reshape(1, kd), norm_w.reshape(1, GLA_HEAD_V), tril)


_DQ_CQ = 0
_DQ_CKV = 512
_DQ_KRA = 768
_DQ_KRB = 896
_DQ_COLS = 1024
_QKV_COLS = MLA_HEADS * HEAD_PAD
V7X_MXU_COLS = 256
_HEAD_GROUP = V7X_MXU_COLS // HEAD_PAD
_GROUP_COLS = _HEAD_GROUP * HEAD_PAD


def _q_up_kernel(cq_ref, lw_ref, wa_ref, wb_ref, ga_ref, gb_ref, cos_ref, sin_ref, ind_ref, indt_ref, o_ref,
                 *, scale):
    c = _rms(cq_ref[...], lw_ref[...]).astype(BF16)
    fa = jnp.concatenate([ga_ref[...] * cos_ref[...] * scale] * _HEAD_GROUP, axis=1)
    fb = jnp.concatenate([gb_ref[...] * sin_ref[...] * scale] * _HEAD_GROUP, axis=1)
    for g in range(MLA_HEADS // _HEAD_GROUP):
        sl = slice(g * _GROUP_COLS, (g + 1) * _GROUP_COLS)
        a = _dot(c, wa_ref[:, sl])
        b = _dot(c, wb_ref[:, sl])
        ms = _dot((a * a).astype(BF16), ind_ref[...])
        r = _dot_sel(lax.rsqrt(ms + NORM_EPS), indt_ref[...])
        o_ref[:, sl] = ((a * fa + b * fb) * r).astype(o_ref.dtype)


def _kv_up_kernel(ckv_ref, kra_ref, krb_ref, lw_ref, wk_ref, wv_ref, gk_ref, gra_ref, grb_ref, cos_ref, sin_ref,
                  ind_ref, indt_ref, ones_ref, k_ref, v_ref):
    c = _rms(ckv_ref[...], lw_ref[...]).astype(BF16)
    kra = kra_ref[...]
    krb = krb_ref[...]
    r_rope = lax.rsqrt(jnp.sum(kra * kra, axis=-1, keepdims=True) * (1.0 / MLA_ROPE) + NORM_EPS)
    k_rope = (kra * (gra_ref[...] * cos_ref[...]) + krb * (grb_ref[...] * sin_ref[...])) * r_rope
    k_rope = jnp.concatenate([k_rope] * _HEAD_GROUP, axis=1)
    gk = jnp.concatenate([gk_ref[...]] * _HEAD_GROUP, axis=1)
    ones = jnp.concatenate([ones_ref[...]] * _HEAD_GROUP, axis=1)
    for g in range(MLA_HEADS // _HEAD_GROUP):
        sl = slice(g * _GROUP_COLS, (g + 1) * _GROUP_COLS)
        kn = _dot(c, wk_ref[:, sl])
        ms = _dot((kn * kn).astype(BF16), ind_ref[...])
        r = _dot_sel(lax.rsqrt(ms + NORM_EPS), indt_ref[...])
        k_ref[:, sl] = (kn * gk * r + k_rope).astype(k_ref.dtype)
        v_ref[:, sl] = (_dot(c, wv_ref[:, sl]) + ones).astype(v_ref.dtype)


def _head_selectors():
    lane = jnp.arange(_GROUP_COLS)
    head, pos = lane // HEAD_PAD, lane % HEAD_PAD
    nope = pos < MLA_NOPE
    rope = (pos >= MLA_NOPE) & (pos < MLA_NOPE + MLA_ROPE)
    col = jnp.arange(LANES)[None, :]
    sel_nope = nope[:, None] & (col == 2 * head[:, None])
    sel_rope = rope[:, None] & (col == 2 * head[:, None] + 1)
    ind = sel_nope.astype(F32) / MLA_NOPE + sel_rope.astype(F32) / MLA_ROPE
    indt = (sel_nope | sel_rope).astype(F32).T
    return ind.astype(BF16), indt.astype(BF16)


def _rope_tables(seqlen, batch):
    pos = jnp.arange(seqlen, dtype=F32)
    inv = 1.0 / (ROPE_THETA ** (jnp.arange(0, MLA_ROPE, 2, dtype=F32) / MLA_ROPE))
    ang = pos[:, None] * inv[None, :]
    cos, sin = jnp.cos(ang), jnp.sin(ang)
    pad = jnp.zeros((seqlen, HEAD_PAD - MLA_NOPE - MLA_ROPE), F32)
    cos_t = jnp.concatenate([jnp.ones((seqlen, MLA_NOPE), F32), cos, cos, pad], axis=1)
    sin_t = jnp.concatenate([jnp.zeros((seqlen, MLA_NOPE), F32), -sin, sin, pad], axis=1)
    return jnp.tile(cos_t, (batch, 1)), jnp.tile(sin_t, (batch, 1))


def _swap_halves(w):
    half = w.shape[-1] // 2
    return jnp.concatenate([w[..., half:], w[..., :half]], axis=-1)


def _rope_lanes(w):
    pad = [(0, 0)] * (w.ndim - 1) + [(MLA_NOPE, HEAD_PAD - MLA_NOPE - MLA_ROPE)]
    return jnp.pad(w, pad)


def q_up(cqkv, lora_norm, w_uq, nope_norm, rope_norm, cos_t, sin_t, *, tm=512):
    t = cqkv.shape[0]
    w = w_uq.reshape(MLA_Q_RANK, MLA_HEADS, MLA_NOPE + MLA_ROPE)
    w_nope, w_rope = w[..., :MLA_NOPE], w[..., MLA_NOPE:]
    pad = jnp.zeros((MLA_Q_RANK, MLA_HEADS, HEAD_PAD - MLA_NOPE - MLA_ROPE), F32)
    wa = jnp.concatenate([w_nope, w_rope, pad], axis=-1).reshape(MLA_Q_RANK, _QKV_COLS).astype(BF16)
    wb = _rope_lanes(_swap_halves(w_rope)).reshape(MLA_Q_RANK, _QKV_COLS).astype(BF16)
    ga = jnp.concatenate([nope_norm, rope_norm, jnp.zeros((HEAD_PAD - MLA_NOPE - MLA_ROPE,), F32)]).reshape(1, HEAD_PAD)
    gb = _rope_lanes(_swap_halves(rope_norm)).reshape(1, HEAD_PAD)
    ind, indt = _head_selectors()
    row = lambda width, j: pl.BlockSpec((tm, width), lambda i: (i, j))
    return pl.pallas_call(
        functools.partial(_q_up_kernel, scale=(MLA_NOPE + MLA_ROPE) ** -0.5 * math.log2(math.e)),
        grid=(t // tm,),
        in_specs=[
            row(MLA_Q_RANK, _DQ_CQ // MLA_Q_RANK),
            _const_spec((1, MLA_Q_RANK)),
            _const_spec(wa.shape), _const_spec(wb.shape),
            _const_spec((1, HEAD_PAD)), _const_spec((1, HEAD_PAD)),
            row(HEAD_PAD, 0), row(HEAD_PAD, 0),
            _const_spec(ind.shape), _const_spec(indt.shape),
        ],
        out_specs=row(_QKV_COLS, 0),
        out_shape=jax.ShapeDtypeStruct((t, _QKV_COLS), BF16),
        compiler_params=_cparams("parallel"),
        name="mla_q_up",
    )(cqkv, lora_norm.reshape(1, MLA_Q_RANK), wa, wb, ga, gb, cos_t, sin_t, ind, indt)


def kv_up(cqkv, lora_norm, w_ukv, nope_norm, rope_norm, cos_t, sin_t, *, tm=512):
    t = cqkv.shape[0]
    w = w_ukv.reshape(MLA_KV_RANK, MLA_HEADS, MLA_NOPE + MLA_V)
    pad_k = [(0, 0), (0, 0), (0, HEAD_PAD - MLA_NOPE)]
    pad_v = [(0, 0), (0, 0), (0, HEAD_PAD - MLA_V)]
    wk = jnp.pad(w[..., :MLA_NOPE], pad_k).reshape(MLA_KV_RANK, _QKV_COLS).astype(BF16)
    wv = jnp.pad(w[..., MLA_NOPE:], pad_v).reshape(MLA_KV_RANK, _QKV_COLS).astype(BF16)
    gk = jnp.pad(nope_norm, (0, HEAD_PAD - MLA_NOPE)).reshape(1, HEAD_PAD)
    gra = _rope_lanes(rope_norm).reshape(1, HEAD_PAD)
    grb = _rope_lanes(_swap_halves(rope_norm)).reshape(1, HEAD_PAD)
    ones = (jnp.arange(HEAD_PAD) >= MLA_V).astype(F32).reshape(1, HEAD_PAD)
    ind, indt = _head_selectors()
    row = lambda width, j: pl.BlockSpec((tm, width), lambda i: (i, j))
    return pl.pallas_call(
        _kv_up_kernel,
        grid=(t // tm,),
        in_specs=[
            row(MLA_KV_RANK, _DQ_CKV // MLA_KV_RANK),
            row(HEAD_PAD, _DQ_KRA // HEAD_PAD), row(HEAD_PAD, _DQ_KRB // HEAD_PAD),
            _const_spec((1, MLA_KV_RANK)),
            _const_spec(wk.shape), _const_spec(wv.shape),
            _const_spec((1, HEAD_PAD)), _const_spec((1, HEAD_PAD)), _const_spec((1, HEAD_PAD)),
            row(HEAD_PAD, 0), row(HEAD_PAD, 0),
            _const_spec(ind.shape), _const_spec(indt.shape), _const_spec((1, HEAD_PAD)),
        ],
        out_specs=[row(_QKV_COLS, 0), row(_QKV_COLS, 0)],
        out_shape=[jax.ShapeDtypeStruct((t, _QKV_COLS), BF16)] * 2,
        compiler_params=_cparams("parallel"),
        name="mla_kv_up",
    )(cqkv, cqkv, cqkv, lora_norm.reshape(1, MLA_KV_RANK), wk, wv, gk, gra, grb, cos_t, sin_t, ind, indt, ones)


def _attn_kernel(q_ref, k_ref, v_ref, o_ref, *, tq, tr):
    qi = pl.program_id(2)
    ns = tq // tr
    q = [q_ref[st * tr:(st + 1) * tr, :] for st in range(ns)]

    def update(st, kb, vb, m, acc, mask=None):
        s = _dot_nt(q[st], kb)
        if mask is not None:
            s = jnp.where(mask, s, -jnp.inf)
        m_new = jnp.maximum(m, jnp.max(s, axis=-1, keepdims=True))
        p = jnp.exp2(s - m_new)
        acc = jnp.exp2(m - m_new) * acc + _dot(p.astype(BF16), vb)
        return m_new, acc

    def body(j, carry):
        off = pl.multiple_of(j * tq, tq)
        kb = k_ref[pl.ds(off, tq), :]
        vb = v_ref[pl.ds(off, tq), :]
        return tuple(update(st, kb, vb, *carry[st]) for st in range(ns))

    init = tuple((jnp.full((tr, 1), -jnp.inf, F32), jnp.zeros((tr, HEAD_PAD), F32)) for _ in range(ns))
    state = lax.fori_loop(0, qi, body, init)
    off = pl.multiple_of(qi * tq, tq)
    kb = k_ref[pl.ds(off, tq), :]
    vb = v_ref[pl.ds(off, tq), :]
    row = lax.broadcasted_iota(jnp.int32, (tr, tq), 0)
    col = lax.broadcasted_iota(jnp.int32, (tr, tq), 1)
    for st in range(ns):
        _, acc = update(st, kb, vb, *state[st], mask=col <= row + st * tr)
        o_ref[st * tr:(st + 1) * tr, :] = (acc / acc[:, MLA_V:MLA_V + 1]).astype(o_ref.dtype)


def flash_attention(q, k, v, *, batch, tq=1024, tr=512):
    t = q.shape[0]
    seqlen = t // batch
    nq = seqlen // tq
    return pl.pallas_call(
        functools.partial(_attn_kernel, tq=tq, tr=tr),
        grid=(batch, MLA_HEADS, nq),
        in_specs=[
            pl.BlockSpec((tq, HEAD_PAD), lambda b, h, i: (b * nq + i, h)),
            pl.BlockSpec((seqlen, HEAD_PAD), lambda b, h, i: (b, h)),
            pl.BlockSpec((seqlen, HEAD_PAD), lambda b, h, i: (b, h)),
        ],
        out_specs=pl.BlockSpec((tq, HEAD_PAD), lambda b, h, i: (b * nq + i, h)),
        out_shape=jax.ShapeDtypeStruct((t, _QKV_COLS), BF16),
        compiler_params=_cparams("parallel", "parallel", "arbitrary"),
        name="mla_attention",
    )(q, k, v)


def _even_layer(x, batch, norm_w, w_in, conv_w, conv_b, dt_bias, a_log, d_skip, ssd_norm,
                gk_w2, gk_b, gla_norm, w_out):
    sizes = (D_SSM, CONV_DIM, SSD_HEADS, GLA_KEY_DIM, GLA_KEY_DIM, GLA_VAL_DIM, GLA_VAL_DIM, GLA_GATE_RANK)
    offs = [0]
    for s in sizes:
        offs.append(offs[-1] + s)
    seg = [w_in[:, offs[i]:offs[i + 1]] for i in range(len(sizes))]
    w_z, w_xbc, w_dt, w_q, w_k, w_v, w_g, w_gkl = seg
    w_main = jnp.concatenate(
        [w_z, w_v, w_g, w_xbc, w_q, w_k, jnp.pad(w_gkl, ((0, 0), (0, LANES - GLA_GATE_RANK)))], axis=1).astype(BF16)
    w_dtp = jnp.pad(w_dt, ((0, 0), (0, LANES - SSD_HEADS))).astype(BF16)
    proj, dt_raw = norm_proj(x, norm_w, [w_main, w_dtp], [BF16, F32])
    y_ssd = ssd_scan(proj, dt_raw, conv_w, conv_b, dt_bias, a_log, d_skip, ssd_norm, batch=batch)
    y_gla = gla_scan(proj, gk_w2, gk_b, gla_norm, batch=batch)
    w_o = w_out.astype(BF16)
    return proj_residual(x, [y_ssd, y_gla], [w_o[:D_SSM], w_o[D_SSM:]])


def _odd_layer(x, batch, cos_t, sin_t, norm_w, w_dqkv, q_lora_norm, w_uq, kv_lora_norm, w_ukv,
               q_nope_norm, q_rope_norm, k_nope_norm, k_rope_norm, w_o):
    w_cq = w_dqkv[:, :MLA_Q_RANK]
    w_ckv = w_dqkv[:, MLA_Q_RANK:MLA_Q_RANK + MLA_KV_RANK]
    w_kr = w_dqkv[:, MLA_Q_RANK + MLA_KV_RANK:]
    w_d = jnp.concatenate(
        [w_cq, jnp.zeros((D_MODEL, _DQ_CKV - MLA_Q_RANK), F32), w_ckv, _rope_lanes(w_kr),
         _rope_lanes(_swap_halves(w_kr))], axis=1).astype(BF16)
    (cqkv,) = norm_proj(x, norm_w, [w_d], [F32])
    q = q_up(cqkv, q_lora_norm, w_uq, q_nope_norm, q_rope_norm, cos_t, sin_t)
    k, v = kv_up(cqkv, kv_lora_norm, w_ukv, k_nope_norm, k_rope_norm, cos_t, sin_t)
    o = flash_attention(q, k, v, batch=batch)
    w_oe = jnp.pad(w_o.reshape(MLA_HEADS, MLA_V, D_MODEL), ((0, 0), (0, HEAD_PAD - MLA_V), (0, 0)))
    return proj_residual(x, [o], [w_oe.reshape(_QKV_COLS, D_MODEL).astype(BF16)])


def kernel(x, mix_norm_even, w_in_even, conv_w, conv_b, dt_bias, a_log, d_skip, ssd_norm, gla_gk_w2, gla_gk_b, gla_norm, w_out_even, mix_norm_odd, w_dqkv, q_lora_norm, w_uq, kv_lora_norm, w_ukv, q_nope_norm, q_rope_norm, k_nope_norm, k_rope_norm, w_o_mla, ffn_norm, w_gate, w_up, w_down):
    batch, seqlen, d = x.shape
    depth = ffn_norm.shape[0]
    cos_t, sin_t = _rope_tables(seqlen, batch)
    x = x.reshape(batch * seqlen, d)
    for i in range(depth):
        j = i // 2
        if i % 2 == 0:
            x = _even_layer(x, batch, mix_norm_even[j], w_in_even[j], conv_w[j], conv_b[j], dt_bias[j], a_log[j],
                            d_skip[j], ssd_norm[j], gla_gk_w2[j], gla_gk_b[j], gla_norm[j], w_out_even[j])
        else:
            x = _odd_layer(x, batch, cos_t, sin_t, mix_norm_odd[j], w_dqkv[j], q_lora_norm[j], w_uq[j],
                           kv_lora_norm[j], w_ukv[j], q_nope_norm[j], q_rope_norm[j], k_nope_norm[j],
                           k_rope_norm[j], w_o_mla[j])
        x = ffn_residual(x, ffn_norm[i], w_gate[i].astype(BF16), w_up[i].astype(BF16), w_down[i].astype(BF16))
    return x.reshape(batch, seqlen, d)
```

```python
import functools
import math

import jax
import jax.numpy as jnp
from jax import lax
from jax.experimental import pallas as pl
from jax.experimental.pallas import tpu as pltpu

F32 = jnp.float32
BF16 = jnp.bfloat16

D_MODEL = 1024
NORM_EPS = 1e-6

SSD_HEAD_DIM = 64
SSD_HEADS = 16
SSD_GROUPS = 2
SSD_HPG = SSD_HEADS // SSD_GROUPS
SSD_STATE = 128
SSD_CHUNK = 128
CONV_WIDTH = 4
D_SSM = SSD_HEADS * SSD_HEAD_DIM
CONV_DIM = D_SSM + 2 * SSD_GROUPS * SSD_STATE

GLA_HEADS = 4
GLA_KEY_DIM = D_MODEL // 2
GLA_VAL_DIM = D_MODEL
GLA_HEAD_K = GLA_KEY_DIM // GLA_HEADS
GLA_HEAD_V = GLA_VAL_DIM // GLA_HEADS
GLA_GATE_RANK = 16
GLA_GATE_NORM = 16.0
GLA_CHUNK = 64

MLA_HEADS = 16
MLA_NOPE = 64
MLA_ROPE = 32
MLA_V = 64
MLA_Q_RANK = 384
MLA_KV_RANK = 256
ROPE_THETA = 10000.0

LANES = 128
HEAD_PAD = 128
V7X_VMEM_BYTES = 64 * 1024 * 1024
VMEM_LIMIT = V7X_VMEM_BYTES - 8 * 1024 * 1024


def _cparams(*sem):
    return pltpu.CompilerParams(dimension_semantics=sem, vmem_limit_bytes=VMEM_LIMIT)


def _dot(a, b):
    return jnp.dot(a, b, preferred_element_type=F32)


def _dot_nt(a, b):
    return lax.dot_general(a, b, (((1,), (1,)), ((), ())), preferred_element_type=F32)


def _dot_tn(a, b):
    return lax.dot_general(a, b, (((0,), (0,)), ((), ())), preferred_element_type=F32)


def _split3(x):
    hi = x.astype(BF16)
    r1 = x - hi.astype(F32)
    mid = r1.astype(BF16)
    lo = (r1 - mid.astype(F32)).astype(BF16)
    return hi, mid, lo


def _dot_sel(x, m):
    hi, mid, lo = _split3(x)
    return _dot(hi, m) + _dot(mid, m) + _dot(lo, m)


def _sel_dot(m, x):
    hi, mid, lo = _split3(x)
    return _dot(m, hi) + _dot(m, mid) + _dot(m, lo)


def _rms(x, w):
    return x * lax.rsqrt(jnp.mean(x * x, axis=-1, keepdims=True) + NORM_EPS) * w


def _silu(x):
    return x * jax.nn.sigmoid(x)


def _softplus(x):
    return jnp.maximum(x, 0.0) + jnp.log1p(jnp.exp(-jnp.abs(x)))


def _col_chunks(n, step):
    return [(c, min(c + step, n)) for c in range(0, n, step)]


def _const_spec(shape):
    return pl.BlockSpec(shape, lambda *_: (0,) * len(shape))


def _norm_proj_kernel(*refs, n_out, chunk):
    x_ref, nw_ref = refs[0], refs[1]
    w_refs = refs[2:2 + n_out]
    o_refs = refs[2 + n_out:]
    h = _rms(x_ref[...].astype(F32), nw_ref[...]).astype(BF16)
    for w_ref, o_ref in zip(w_refs, o_refs):
        for c0, c1 in _col_chunks(w_ref.shape[1], chunk):
            o_ref[:, c0:c1] = _dot(h, w_ref[:, c0:c1]).astype(o_ref.dtype)


def norm_proj(x, norm_w, weights, out_dtypes, *, tm=512, chunk=512):
    t, k = x.shape
    n_out = len(weights)
    in_specs = [pl.BlockSpec((tm, k), lambda i: (i, 0)), _const_spec((1, k))]
    in_specs += [_const_spec(w.shape) for w in weights]
    out_specs = [pl.BlockSpec((tm, w.shape[1]), lambda i: (i, 0)) for w in weights]
    out_shape = [jax.ShapeDtypeStruct((t, w.shape[1]), dt) for w, dt in zip(weights, out_dtypes)]
    return pl.pallas_call(
        functools.partial(_norm_proj_kernel, n_out=n_out, chunk=chunk),
        grid=(t // tm,),
        in_specs=in_specs,
        out_specs=out_specs,
        out_shape=out_shape,
        compiler_params=_cparams("parallel"),
        name="norm_proj",
    )(x, norm_w.reshape(1, k), *weights)


def _proj_res_kernel(*refs, n_in):
    x_ref = refs[0]
    a_refs = refs[1:1 + n_in]
    w_refs = refs[1 + n_in:1 + 2 * n_in]
    o_ref = refs[1 + 2 * n_in]
    acc = x_ref[...]
    for a_ref, w_ref in zip(a_refs, w_refs):
        acc = acc + _dot(a_ref[...], w_ref[...])
    o_ref[...] = acc


def proj_residual(x, acts, weights, *, tm=512):
    t, d = x.shape
    n_in = len(acts)
    in_specs = [pl.BlockSpec((tm, d), lambda i: (i, 0))]
    in_specs += [pl.BlockSpec((tm, a.shape[1]), lambda i: (i, 0)) for a in acts]
    in_specs += [_const_spec(w.shape) for w in weights]
    return pl.pallas_call(
        functools.partial(_proj_res_kernel, n_in=n_in),
        grid=(t // tm,),
        in_specs=in_specs,
        out_specs=pl.BlockSpec((tm, d), lambda i: (i, 0)),
        out_shape=jax.ShapeDtypeStruct((t, d), F32),
        compiler_params=_cparams("parallel"),
        name="proj_residual",
    )(x, *acts, *weights)


def _ffn_kernel(x_ref, nw_ref, wg_ref, wu_ref, wd_ref, o_ref, *, chunk):
    x = x_ref[...]
    h = _rms(x, nw_ref[...]).astype(BF16)
    acc = x
    for c0, c1 in _col_chunks(wg_ref.shape[1], chunk):
        g = _dot(h, wg_ref[:, c0:c1])
        u = _dot(h, wu_ref[:, c0:c1])
        a = (_silu(g) * u).astype(BF16)
        acc = acc + _dot(a, wd_ref[c0:c1, :])
    o_ref[...] = acc


def ffn_residual(x, norm_w, wg, wu, wd, *, tm=512, chunk=256):
    t, d = x.shape
    row = pl.BlockSpec((tm, d), lambda i: (i, 0))
    return pl.pallas_call(
        functools.partial(_ffn_kernel, chunk=chunk),
        grid=(t // tm,),
        in_specs=[row, _const_spec((1, d)), _const_spec(wg.shape), _const_spec(wu.shape), _const_spec(wd.shape)],
        out_specs=row,
        out_shape=jax.ShapeDtypeStruct((t, d), F32),
        compiler_params=_cparams("parallel"),
        name="ffn_residual",
    )(x, norm_w.reshape(1, d), wg, wu, wd)


_COL_Z = 0
_COL_V = _COL_Z + D_SSM
_COL_G = _COL_V + GLA_VAL_DIM
_COL_XBC = _COL_G + GLA_VAL_DIM
_COL_Q = _COL_XBC + CONV_DIM
_COL_K = _COL_Q + GLA_KEY_DIM
_COL_GKL = _COL_K + GLA_KEY_DIM
_PROJ_COLS = _COL_GKL + LANES
_CONV_HALO = 8


def _ssd_kernel(z_ref, xs_ref, bc_ref, dt_ref, cw_ref, cb_ref, dtb_ref, nega_ref, dskip_ref, nw_ref,
                tril_ref, exp_ref, rep_ref, o_ref, xpad_ref, state_ref):
    q = SSD_CHUNK
    p = SSD_HEAD_DIM
    n = SSD_STATE
    gw = SSD_HPG * p

    @pl.when(pl.program_id(1) == 0)
    def _():
        xpad_ref[0:_CONV_HALO, :] = jnp.zeros((_CONV_HALO, CONV_DIM), F32)
        state_ref[...] = jnp.zeros_like(state_ref)

    xpad_ref[_CONV_HALO:, 0:D_SSM] = xs_ref[...].astype(F32)
    xpad_ref[_CONV_HALO:, D_SSM:] = bc_ref[...].astype(F32)
    conv = cb_ref[...]
    for j in range(CONV_WIDTH):
        off = _CONV_HALO - (CONV_WIDTH - 1) + j
        conv = conv + cw_ref[j:j + 1, :] * xpad_ref[off:off + q, :]
    xpad_ref[0:_CONV_HALO, :] = xpad_ref[q:q + _CONV_HALO, :]
    xbc = _silu(conv)

    dt = _softplus(dt_ref[...] + dtb_ref[...])
    a = dt * nega_ref[...]
    acum = _sel_dot(tril_ref[...], a)
    acum_t = acum.T
    dt_x = _dot_sel(dt, exp_ref[...])
    acum_x = _dot_sel(acum, exp_ref[...])
    acum_rep = _dot_sel(acum, rep_ref[...])
    a_last_x = acum_x[q - 1:q, :]
    xs = xbc[:, 0:D_SSM]
    xdt = xs * dt_x
    xdt_bf = xdt.astype(BF16)
    w_end = (xdt * jnp.exp(a_last_x - acum_x)).astype(BF16)
    exp_a = jnp.exp(acum_x)
    chunk_decay = jnp.exp(a_last_x)
    y_skip = xs * dskip_ref[...]
    row = lax.broadcasted_iota(jnp.int32, (q, q), 0)
    col = lax.broadcasted_iota(jnp.int32, (q, q), 1)
    causal = row >= col
    lane = lax.broadcasted_iota(jnp.int32, (q, 2 * p), 1)

    y_groups = []
    for g in range(SSD_GROUPS):
        gs = slice(g * gw, (g + 1) * gw)
        b_g = xbc[:, D_SSM + g * n:D_SSM + (g + 1) * n]
        c_g = xbc[:, D_SSM + (SSD_GROUPS + g) * n:D_SSM + (SSD_GROUPS + g + 1) * n]
        b_bf = b_g.astype(BF16)
        c_bf = c_g.astype(BF16)
        cb = _dot_nt(c_bf, b_bf)
        s_g = state_ref[g]
        y_off = _dot(c_bf, s_g.astype(BF16)) * exp_a[:, gs]
        y_pairs = []
        for k in range(0, SSD_HPG, 2):
            lhs = []
            for hk in (g * SSD_HPG + k, g * SSD_HPG + k + 1):
                a_col = acum_rep[:, hk * q:(hk + 1) * q]
                a_row = acum_t[hk:hk + 1, :]
                decay = jnp.where(causal, jnp.exp(jnp.minimum(a_col - a_row, 0.0)), 0.0)
                lhs.append((cb * decay).astype(BF16))
            hk0 = g * SSD_HPG + k
            x_pair = xdt_bf[:, hk0 * p:(hk0 + 2) * p]
            zero = jnp.zeros_like(x_pair)
            rhs = jnp.concatenate([jnp.where(lane < p, x_pair, zero), jnp.where(lane >= p, x_pair, zero)], axis=0)
            y_pairs.append(_dot(jnp.concatenate(lhs, axis=1), rhs))
        state_ref[g] = s_g * chunk_decay[:, gs] + _dot_tn(b_bf, w_end[:, gs])
        y_g = jnp.concatenate(y_pairs, axis=1) + y_off + y_skip[:, gs]
        y_g = y_g * _silu(z_ref[:, gs].astype(F32))
        y_groups.append(_rms(y_g, nw_ref[:, gs]))
    o_ref[...] = jnp.concatenate(y_groups, axis=1).astype(o_ref.dtype)


def ssd_scan(proj, dt_raw, conv_w, conv_b, dt_bias, a_log, d_skip, norm_w, *, batch):
    t = proj.shape[0]
    q = SSD_CHUNK
    nc = t // batch // q

    def pad_heads(v):
        return jnp.pad(v.reshape(1, SSD_HEADS), ((0, 0), (0, LANES - SSD_HEADS)))

    tril = jnp.tril(jnp.ones((q, q), F32)).astype(BF16)
    head = jnp.arange(LANES)[:, None]
    expand = (head == jnp.arange(D_SSM)[None, :] // SSD_HEAD_DIM).astype(BF16)
    replicate = (head == jnp.arange(SSD_HEADS * q)[None, :] // q).astype(BF16)
    rows = lambda b, c: b * nc + c
    return pl.pallas_call(
        _ssd_kernel,
        grid=(batch, nc),
        in_specs=[
            pl.BlockSpec((q, D_SSM), lambda b, c: (rows(b, c), _COL_Z // D_SSM)),
            pl.BlockSpec((q, D_SSM), lambda b, c: (rows(b, c), _COL_XBC // D_SSM)),
            pl.BlockSpec((q, CONV_DIM - D_SSM), lambda b, c: (rows(b, c), (_COL_XBC + D_SSM) // (CONV_DIM - D_SSM))),
            pl.BlockSpec((q, LANES), lambda b, c: (rows(b, c), 0)),
            _const_spec((CONV_WIDTH, CONV_DIM)),
            _const_spec((1, CONV_DIM)),
            _const_spec((1, LANES)),
            _const_spec((1, LANES)),
            _const_spec((1, D_SSM)),
            _const_spec((1, D_SSM)),
            _const_spec((q, q)),
            _const_spec(expand.shape),
            _const_spec(replicate.shape),
        ],
        out_specs=pl.BlockSpec((q, D_SSM), lambda b, c: (rows(b, c), 0)),
        out_shape=jax.ShapeDtypeStruct((t, D_SSM), BF16),
        scratch_shapes=[
            pltpu.VMEM((q + _CONV_HALO, CONV_DIM), F32),
            pltpu.VMEM((SSD_GROUPS, SSD_STATE, SSD_HPG * SSD_HEAD_DIM), F32),
        ],
        compiler_params=_cparams("parallel", "arbitrary"),
        name="ssd_scan",
    )(proj, proj, proj, dt_raw, conv_w, conv_b.reshape(1, CONV_DIM), pad_heads(dt_bias),
      pad_heads(-jnp.exp(a_log)), jnp.repeat(d_skip, SSD_HEAD_DIM).reshape(1, D_SSM), norm_w.reshape(1, D_SSM),
      tril, expand, replicate)


def _gla_kernel(q_ref, k_ref, v_ref, g_ref, gkl_ref, w2_ref, gkb_ref, nw_ref, tril_ref, o_ref, state_ref,
                *, block):
    c = GLA_CHUNK
    dk, dv = GLA_HEAD_K, GLA_HEAD_V
    scale = dk ** -0.5

    @pl.when(pl.program_id(1) == 0)
    def _():
        state_ref[...] = jnp.zeros_like(state_ref)

    gk = -_softplus(-(_dot(gkl_ref[...], w2_ref[...]) + gkb_ref[...])) * (1.0 / GLA_GATE_NORM)
    row = lax.broadcasted_iota(jnp.int32, (c, c), 0)
    col = lax.broadcasted_iota(jnp.int32, (c, c), 1)
    causal = row >= col
    for j in range(block // c):
        r0, r1 = j * c, (j + 1) * c
        gcum_all = _sel_dot(tril_ref[...], gk[r0:r1])
        for h in range(GLA_HEADS):
            ks, vs = slice(h * dk, (h + 1) * dk), slice(h * dv, (h + 1) * dv)
            gcum = gcum_all[:, ks]
            g_last = gcum[c - 1:c, :]
            qc = q_ref[r0:r1, ks].astype(F32) * scale
            kc = k_ref[r0:r1, ks].astype(F32)
            vc = v_ref[r0:r1, vs]
            q_dec = (qc * jnp.exp(gcum)).astype(BF16)
            k_inv = (kc * jnp.exp(-gcum)).astype(BF16)
            k_end = (kc * jnp.exp(g_last - gcum)).astype(BF16)
            scores = jnp.where(causal, _dot_nt(q_dec, k_inv), 0.0)
            s_t = state_ref[h]
            o = _dot(scores.astype(BF16), vc) + _dot_nt(q_dec, s_t.astype(BF16))
            state_ref[h] = s_t * jnp.exp(g_last) + _dot_tn(vc, k_end)
            o = _rms(o, nw_ref[...]) * _silu(g_ref[r0:r1, vs].astype(F32))
            o_ref[r0:r1, vs] = o.astype(o_ref.dtype)


def gla_scan(proj, gk_w2, gk_b, norm_w, *, batch, block=256):
    t = proj.shape[0]
    nb = t // batch // block
    kd, vd = GLA_KEY_DIM, GLA_VAL_DIM
    w2 = jnp.pad(gk_w2, ((0, LANES - GLA_GATE_RANK), (0, 0))).astype(BF16)
    tril = jnp.tril(jnp.ones((GLA_CHUNK, GLA_CHUNK), F32)).astype(BF16)
    rows = lambda b, i: b * nb + i
    return pl.pallas_call(
        functools.partial(_gla_kernel, block=block),
        grid=(batch, nb),
        in_specs=[
            pl.BlockSpec((block, kd), lambda b, i: (rows(b, i), _COL_Q // kd)),
            pl.BlockSpec((block, kd), lambda b, i: (rows(b, i), _COL_K // kd)),
            pl.BlockSpec((block, vd), lambda b, i: (rows(b, i), _COL_V // vd)),
            pl.BlockSpec((block, vd), lambda b, i: (rows(b, i), _COL_G // vd)),
            pl.BlockSpec((block, LANES), lambda b, i: (rows(b, i), _COL_GKL // LANES)),
            _const_spec((LANES, kd)),
            _const_spec((1, kd)),
            _const_spec((1, GLA_HEAD_V)),
            _const_spec((GLA_CHUNK, GLA_CHUNK)),
        ],
        out_specs=pl.BlockSpec((block, vd), lambda b, i: (rows(b, i), 0)),
        out_shape=jax.ShapeDtypeStruct((t, vd), BF16),
        scratch_shapes=[pltpu.VMEM((GLA_HEADS, GLA_HEAD_V, GLA_HEAD_K), F32)],
        compiler_params=_cparams("parallel", "arbitrary"),
        name="gla_scan",
    )(proj, proj, proj, proj, proj, w2, gk_b.reshape(1, kd), norm_w.reshape(1, GLA_HEAD_V), tril)


_DQ_CQ = 0
_DQ_CKV = 512
_DQ_KRA = 768
_DQ_KRB = 896
_DQ_COLS = 1024
_QKV_COLS = MLA_HEADS * HEAD_PAD
V7X_MXU_COLS = 256
_HEAD_GROUP = V7X_MXU_COLS // HEAD_PAD
_GROUP_COLS = _HEAD_GROUP * HEAD_PAD


def _q_up_kernel(cq_ref, lw_ref, wa_ref, wb_ref, ga_ref, gb_ref, cos_ref, sin_ref, ind_ref, indt_ref, o_ref,
                 *, scale):
    c = _rms(cq_ref[...], lw_ref[...]).astype(BF16)
    fa = jnp.concatenate([ga_ref[...] * cos_ref[...] * scale] * _HEAD_GROUP, axis=1)
    fb = jnp.concatenate([gb_ref[...] * sin_ref[...] * scale] * _HEAD_GROUP, axis=1)
    for g in range(MLA_HEADS // _HEAD_GROUP):
        sl = slice(g * _GROUP_COLS, (g + 1) * _GROUP_COLS)
        a = _dot(c, wa_ref[:, sl])
        b = _dot(c, wb_ref[:, sl])
        ms = _dot((a * a).astype(BF16), ind_ref[...])
        r = _dot_sel(lax.rsqrt(ms + NORM_EPS), indt_ref[...])
        o_ref[:, sl] = ((a * fa + b * fb) * r).astype(o_ref.dtype)


def _kv_up_kernel(ckv_ref, kra_ref, krb_ref, lw_ref, wk_ref, wv_ref, gk_ref, gra_ref, grb_ref, cos_ref, sin_ref,
                  ind_ref, indt_ref, ones_ref, k_ref, v_ref):
    c = _rms(ckv_ref[...], lw_ref[...]).astype(BF16)
    kra = kra_ref[...]
    krb = krb_ref[...]
    r_rope = lax.rsqrt(jnp.sum(kra * kra, axis=-1, keepdims=True) * (1.0 / MLA_ROPE) + NORM_EPS)
    k_rope = (kra * (gra_ref[...] * cos_ref[...]) + krb * (grb_ref[...] * sin_ref[...])) * r_rope
    k_rope = jnp.concatenate([k_rope] * _HEAD_GROUP, axis=1)
    gk = jnp.concatenate([gk_ref[...]] * _HEAD_GROUP, axis=1)
    ones = jnp.concatenate([ones_ref[...]] * _HEAD_GROUP, axis=1)
    for g in range(MLA_HEADS // _HEAD_GROUP):
        sl = slice(g * _GROUP_COLS, (g + 1) * _GROUP_COLS)
        kn = _dot(c, wk_ref[:, sl])
        ms = _dot((kn * kn).astype(BF16), ind_ref[...])
        r = _dot_sel(lax.rsqrt(ms + NORM_EPS), indt_ref[...])
        k_ref[:, sl] = (kn * gk * r + k_rope).astype(k_ref.dtype)
        v_ref[:, sl] = (_dot(c, wv_ref[:, sl]) + ones).astype(v_ref.dtype)


def _head_selectors():
    lane = jnp.arange(_GROUP_COLS)
    head, pos = lane // HEAD_PAD, lane % HEAD_PAD
    nope = pos < MLA_NOPE
    rope = (pos >= MLA_NOPE) & (pos < MLA_NOPE + MLA_ROPE)
    col = jnp.arange(LANES)[None, :]
    sel_nope = nope[:, None] & (col == 2 * head[:, None])
    sel_rope = rope[:, None] & (col == 2 * head[:, None] + 1)
    ind = sel_nope.astype(F32) / MLA_NOPE + sel_rope.astype(F32) / MLA_ROPE
    indt = (sel_nope | sel_rope).astype(F32).T
    return ind.astype(BF16), indt.astype(BF16)


def _rope_tables(seqlen, batch):
    pos = jnp.arange(seqlen, dtype=F32)
    inv = 1.0 / (ROPE_THETA ** (jnp.arange(0, MLA_ROPE, 2, dtype=F32) / MLA_ROPE))
    ang = pos[:, None] * inv[None, :]
    cos, sin = jnp.cos(ang), jnp.sin(ang)
    pad = jnp.zeros((seqlen, HEAD_PAD - MLA_NOPE - MLA_ROPE), F32)
    cos_t = jnp.concatenate([jnp.ones((seqlen, MLA_NOPE), F32), cos, cos, pad], axis=1)
    sin_t = jnp.concatenate([jnp.zeros((seqlen, MLA_NOPE), F32), -sin, sin, pad], axis=1)
    return jnp.tile(cos_t, (batch, 1)), jnp.tile(sin_t, (batch, 1))


def _swap_halves(w):
    half = w.shape[-1] // 2
    return jnp.concatenate([w[..., half:], w[..., :half]], axis=-1)


def _rope_lanes(w):
    pad = [(0, 0)] * (w.ndim - 1) + [(MLA_NOPE, HEAD_PAD - MLA_NOPE - MLA_ROPE)]
    return jnp.pad(w, pad)


def q_up(cqkv, lora_norm, w_uq, nope_norm, rope_norm, cos_t, sin_t, *, tm=512):
    t = cqkv.shape[0]
    w = w_uq.reshape(MLA_Q_RANK, MLA_HEADS, MLA_NOPE + MLA_ROPE)
    w_nope, w_rope = w[..., :MLA_NOPE], w[..., MLA_NOPE:]
    pad = jnp.zeros((MLA_Q_RANK, MLA_HEADS, HEAD_PAD - MLA_NOPE - MLA_ROPE), F32)
    wa = jnp.concatenate([w_nope, w_rope, pad], axis=-1).reshape(MLA_Q_RANK, _QKV_COLS).astype(BF16)
    wb = _rope_lanes(_swap_halves(w_rope)).reshape(MLA_Q_RANK, _QKV_COLS).astype(BF16)
    ga = jnp.concatenate([nope_norm, rope_norm, jnp.zeros((HEAD_PAD - MLA_NOPE - MLA_ROPE,), F32)]).reshape(1, HEAD_PAD)
    gb = _rope_lanes(_swap_halves(rope_norm)).reshape(1, HEAD_PAD)
    ind, indt = _head_selectors()
    row = lambda width, j: pl.BlockSpec((tm, width), lambda i: (i, j))
    return pl.pallas_call(
        functools.partial(_q_up_kernel, scale=(MLA_NOPE + MLA_ROPE) ** -0.5 * math.log2(math.e)),
        grid=(t // tm,),
        in_specs=[
            row(MLA_Q_RANK, _DQ_CQ // MLA_Q_RANK),
            _const_spec((1, MLA_Q_RANK)),
            _const_spec(wa.shape), _const_spec(wb.shape),
            _const_spec((1, HEAD_PAD)), _const_spec((1, HEAD_PAD)),
            row(HEAD_PAD, 0), row(HEAD_PAD, 0),
            _const_spec(ind.shape), _const_spec(indt.shape),
        ],
        out_specs=row(_QKV_COLS, 0),
        out_shape=jax.ShapeDtypeStruct((t, _QKV_COLS), BF16),
        compiler_params=_cparams("parallel"),
        name="mla_q_up",
    )(cqkv, lora_norm.reshape(1, MLA_Q_RANK), wa, wb, ga, gb, cos_t, sin_t, ind, indt)


def kv_up(cqkv, lora_norm, w_ukv, nope_norm, rope_norm, cos_t, sin_t, *, tm=512):
    t = cqkv.shape[0]
    w = w_ukv.reshape(MLA_KV_RANK, MLA_HEADS, MLA_NOPE + MLA_V)
    pad_k = [(0, 0), (0, 0), (0, HEAD_PAD - MLA_NOPE)]
    pad_v = [(0, 0), (0, 0), (0, HEAD_PAD - MLA_V)]
    wk = jnp.pad(w[..., :MLA_NOPE], pad_k).reshape(MLA_KV_RANK, _QKV_COLS).astype(BF16)
    wv = jnp.pad(w[..., MLA_NOPE:], pad_v).reshape(MLA_KV_RANK, _QKV_COLS).astype(BF16)
    gk = jnp.pad(nope_norm, (0, HEAD_PAD - MLA_NOPE)).reshape(1, HEAD_PAD)
    gra = _rope_lanes(rope_norm).reshape(1, HEAD_PAD)
    grb = _rope_lanes(_swap_halves(rope_norm)).reshape(1, HEAD_PAD)
    ones = (jnp.arange(HEAD_PAD) >= MLA_V).astype(F32).reshape(1, HEAD_PAD)
    ind, indt = _head_selectors()
    row = lambda width, j: pl.BlockSpec((tm, width), lambda i: (i, j))
    return pl.pallas_call(
        _kv_up_kernel,
        grid=(t // tm,),
        in_specs=[
            row(MLA_KV_RANK, _DQ_CKV // MLA_KV_RANK),
            row(HEAD_PAD, _DQ_KRA // HEAD_PAD), row(HEAD_PAD, _DQ_KRB // HEAD_PAD),
            _const_spec((1, MLA_KV_RANK)),
            _const_spec(wk.shape), _const_spec(wv.shape),
            _const_spec((1, HEAD_PAD)), _const_spec((1, HEAD_PAD)), _const_spec((1, HEAD_PAD)),
            row(HEAD_PAD, 0), row(HEAD_PAD, 0),
            _const_spec(ind.shape), _const_spec(indt.shape), _const_spec((1, HEAD_PAD)),
        ],
        out_specs=[row(_QKV_COLS, 0), row(_QKV_COLS, 0)],
        out_shape=[jax.ShapeDtypeStruct((t, _QKV_COLS), BF16)] * 2,
        compiler_params=_cparams("parallel"),
        name="mla_kv_up",
    )(cqkv, cqkv, cqkv, lora_norm.reshape(1, MLA_KV_RANK), wk, wv, gk, gra, grb, cos_t, sin_t, ind, indt, ones)


def _attn_kernel(q_ref, k_ref, v_ref, o_ref, s_ref, p_ref, mx_ref, m_ref, alpha_ref, acc_ref, *, tq, tk):
    qi = pl.program_id(2)

    def rows(j):
        return pl.ds(pl.multiple_of(j * tk, tk), tk)

    def stage_a(slot, j, limit=None):
        s = _dot_nt(q_ref[...], k_ref[rows(j), :])
        if limit is not None:
            rel = (lax.broadcasted_iota(jnp.int32, (tq, tk), 1) - lax.broadcasted_iota(jnp.int32, (tq, tk), 0))
            s = jnp.where(rel <= limit, s, -jnp.inf)
        s_ref[slot] = s
        mx_ref[slot] = jnp.broadcast_to(jnp.max(s, axis=-1, keepdims=True), (tq, LANES))

    def stage_b(slot):
        m = m_ref[...]
        m_new = jnp.maximum(m, mx_ref[slot])
        for c in range(0, tk, LANES):
            pb = jnp.exp2(s_ref[slot, :, c:c + LANES] - m_new).astype(BF16)
            p_ref[slot, :, c:c + LANES] = pltpu.bitcast(pb, jnp.uint32)
        m_ref[...] = m_new
        alpha_ref[...] = jnp.exp2(m - m_new)

    def stage_c(slot, j):
        acc_ref[...] = alpha_ref[...] * acc_ref[...] + _dot(pltpu.bitcast(p_ref[slot], BF16), v_ref[rows(j), :])

    def trip(ii, limit):
        j = 2 * ii
        stage_a(1, j + 1)
        stage_c(1, jnp.maximum(j - 1, 0))
        stage_b(0)
        stage_a(0, j + 2, limit=limit)
        stage_c(0, j)
        stage_b(1)

    p_ref[1] = jnp.zeros((tq // 2, tk), jnp.uint32)
    m_ref[...] = jnp.full((tq, LANES), -jnp.inf, F32)
    alpha_ref[...] = jnp.ones((tq, LANES), F32)
    acc_ref[...] = jnp.zeros((tq, HEAD_PAD), F32)
    stage_a(0, 0, limit=qi * tq)

    @pl.loop(0, qi - 1)
    def _(ii):
        trip(ii, None)

    @pl.when(qi > 0)
    def _():
        trip(qi - 1, 0)

    jd = 2 * qi
    stage_a(1, jd + 1, limit=-tk)
    stage_c(1, jnp.maximum(jd - 1, 0))
    stage_b(0)
    stage_c(0, jd)
    stage_b(1)
    stage_c(1, jd + 1)
    acc = acc_ref[...]
    o_ref[...] = (acc / acc[:, MLA_V:MLA_V + 1]).astype(o_ref.dtype)


def flash_attention(q, k, v, *, batch, tk=512):
    t = q.shape[0]
    seqlen = t // batch
    tq = 2 * tk
    nq = seqlen // tq
    return pl.pallas_call(
        functools.partial(_attn_kernel, tq=tq, tk=tk),
        grid=(batch, MLA_HEADS, nq),
        in_specs=[
            pl.BlockSpec((tq, HEAD_PAD), lambda b, h, i: (b * nq + i, h)),
            pl.BlockSpec((seqlen, HEAD_PAD), lambda b, h, i: (b, h)),
            pl.BlockSpec((seqlen, HEAD_PAD), lambda b, h, i: (b, h)),
        ],
        out_specs=pl.BlockSpec((tq, HEAD_PAD), lambda b, h, i: (b * nq + i, h)),
        out_shape=jax.ShapeDtypeStruct((t, _QKV_COLS), BF16),
        scratch_shapes=[
            pltpu.VMEM((2, tq, tk), F32),
            pltpu.VMEM((2, tq // 2, tk), jnp.uint32),
            pltpu.VMEM((2, tq, LANES), F32),
            pltpu.VMEM((tq, LANES), F32),
            pltpu.VMEM((tq, LANES), F32),
            pltpu.VMEM((tq, HEAD_PAD), F32),
        ],
        compiler_params=_cparams("parallel", "parallel", "arbitrary"),
        name="mla_attention",
    )(q, k, v)


def _even_layer(x, batch, norm_w, w_in, conv_w, conv_b, dt_bias, a_log, d_skip, ssd_norm,
                gk_w2, gk_b, gla_norm, w_out):
    sizes = (D_SSM, CONV_DIM, SSD_HEADS, GLA_KEY_DIM, GLA_KEY_DIM, GLA_VAL_DIM, GLA_VAL_DIM, GLA_GATE_RANK)
    offs = [0]
    for s in sizes:
        offs.append(offs[-1] + s)
    seg = [w_in[:, offs[i]:offs[i + 1]] for i in range(len(sizes))]
    w_z, w_xbc, w_dt, w_q, w_k, w_v, w_g, w_gkl = seg
    w_main = jnp.concatenate(
        [w_z, w_v, w_g, w_xbc, w_q, w_k, jnp.pad(w_gkl, ((0, 0), (0, LANES - GLA_GATE_RANK)))], axis=1).astype(BF16)
    w_dtp = jnp.pad(w_dt, ((0, 0), (0, LANES - SSD_HEADS))).astype(BF16)
    proj, dt_raw = norm_proj(x, norm_w, [w_main, w_dtp], [BF16, F32])
    y_ssd = ssd_scan(proj, dt_raw, conv_w, conv_b, dt_bias, a_log, d_skip, ssd_norm, batch=batch)
    y_gla = gla_scan(proj, gk_w2, gk_b, gla_norm, batch=batch)
    w_o = w_out.astype(BF16)
    return proj_residual(x, [y_ssd, y_gla], [w_o[:D_SSM], w_o[D_SSM:]])


def _odd_layer(x, batch, cos_t, sin_t, norm_w, w_dqkv, q_lora_norm, w_uq, kv_lora_norm, w_ukv,
               q_nope_norm, q_rope_norm, k_nope_norm, k_rope_norm, w_o):
    w_cq = w_dqkv[:, :MLA_Q_RANK]
    w_ckv = w_dqkv[:, MLA_Q_RANK:MLA_Q_RANK + MLA_KV_RANK]
    w_kr = w_dqkv[:, MLA_Q_RANK + MLA_KV_RANK:]
    w_d = jnp.concatenate(
        [w_cq, jnp.zeros((D_MODEL, _DQ_CKV - MLA_Q_RANK), F32), w_ckv, _rope_lanes(w_kr),
         _rope_lanes(_swap_halves(w_kr))], axis=1).astype(BF16)
    (cqkv,) = norm_proj(x, norm_w, [w_d], [F32])
    q = q_up(cqkv, q_lora_norm, w_uq, q_nope_norm, q_rope_norm, cos_t, sin_t)
    k, v = kv_up(cqkv, kv_lora_norm, w_ukv, k_nope_norm, k_rope_norm, cos_t, sin_t)
    o = flash_attention(q, k, v, batch=batch)
    w_oe = jnp.pad(w_o.reshape(MLA_HEADS, MLA_V, D_MODEL), ((0, 0), (0, HEAD_PAD - MLA_V), (0, 0)))
    return proj_residual(x, [o], [w_oe.reshape(_QKV_COLS, D_MODEL).astype(BF16)])


def kernel(x, mix_norm_even, w_in_even, conv_w, conv_b, dt_bias, a_log, d_skip, ssd_norm, gla_gk_w2, gla_gk_b, gla_norm, w_out_even, mix_norm_odd, w_dqkv, q_lora_norm, w_uq, kv_lora_norm, w_ukv, q_nope_norm, q_rope_norm, k_nope_norm, k_rope_norm, w_o_mla, ffn_norm, w_gate, w_up, w_down):
    batch, seqlen, d = x.shape
    depth = ffn_norm.shape[0]
    cos_t, sin_t = _rope_tables(seqlen, batch)
    x = x.reshape(batch * seqlen, d)
    for i in range(depth):
        j = i // 2
        if i % 2 == 0:
            x = _even_layer(x, batch, mix_norm_even[j], w_in_even[j], conv_w[j], conv_b[j], dt_bias[j], a_log[j],
                            d_skip[j], ssd_norm[j], gla_gk_w2[j], gla_gk_b[j], gla_norm[j], w_out_even[j])
        else:
            x = _odd_layer(x, batch, cos_t, sin_t, mix_norm_odd[j], w_dqkv[j], q_lora_norm[j], w_uq[j],
                           kv_lora_norm[j], w_ukv[j], q_nope_norm[j], q_rope_norm[j], k_nope_norm[j],
                           k_rope_norm[j], w_o_mla[j])
        x = ffn_residual(x, ffn_norm[i], w_gate[i].astype(BF16), w_up[i].astype(BF16), w_down[i].astype(BF16))
    return x.reshape(batch, seqlen, d)
```

```python
import functools
import math

import jax
import jax.numpy as jnp
from jax import lax
from jax.experimental import pallas as pl
from jax.experimental.pallas import tpu as pltpu

F32 = jnp.float32
BF16 = jnp.bfloat16

D_MODEL = 1024
NORM_EPS = 1e-6

SSD_HEAD_DIM = 64
SSD_HEADS = 16
SSD_GROUPS = 2
SSD_HPG = SSD_HEADS // SSD_GROUPS
SSD_STATE = 128
SSD_CHUNK = 128
CONV_WIDTH = 4
D_SSM = SSD_HEADS * SSD_HEAD_DIM
CONV_DIM = D_SSM + 2 * SSD_GROUPS * SSD_STATE

GLA_HEADS = 4
GLA_KEY_DIM = D_MODEL // 2
GLA_VAL_DIM = D_MODEL
GLA_HEAD_K = GLA_KEY_DIM // GLA_HEADS
GLA_HEAD_V = GLA_VAL_DIM // GLA_HEADS
GLA_GATE_RANK = 16
GLA_GATE_NORM = 16.0
GLA_CHUNK = 64

MLA_HEADS = 16
MLA_NOPE = 64
MLA_ROPE = 32
MLA_V = 64
MLA_Q_RANK = 384
MLA_KV_RANK = 256
ROPE_THETA = 10000.0

LANES = 128
HEAD_PAD = 128
V7X_VMEM_BYTES = 64 * 1024 * 1024
VMEM_LIMIT = V7X_VMEM_BYTES - 8 * 1024 * 1024


def _cparams(*sem):
    return pltpu.CompilerParams(dimension_semantics=sem, vmem_limit_bytes=VMEM_LIMIT)


def _dot(a, b):
    return jnp.dot(a, b, preferred_element_type=F32)


def _dot_nt(a, b):
    return lax.dot_general(a, b, (((1,), (1,)), ((), ())), preferred_element_type=F32)


def _dot_tn(a, b):
    return lax.dot_general(a, b, (((0,), (0,)), ((), ())), preferred_element_type=F32)


def _split3(x):
    hi = x.astype(BF16)
    r1 = x - hi.astype(F32)
    mid = r1.astype(BF16)
    lo = (r1 - mid.astype(F32)).astype(BF16)
    return hi, mid, lo


def _dot_sel(x, m):
    hi, mid, lo = _split3(x)
    return _dot(hi, m) + _dot(mid, m) + _dot(lo, m)


def _sel_dot(m, x):
    hi, mid, lo = _split3(x)
    return _dot(m, hi) + _dot(m, mid) + _dot(m, lo)


def _rms(x, w):
    return x * lax.rsqrt(jnp.mean(x * x, axis=-1, keepdims=True) + NORM_EPS) * w


def _silu(x):
    return x * jax.nn.sigmoid(x)


def _softplus(x):
    return jnp.maximum(x, 0.0) + jnp.log1p(jnp.exp(-jnp.abs(x)))


def _col_chunks(n, step):
    return [(c, min(c + step, n)) for c in range(0, n, step)]


def _const_spec(shape):
    return pl.BlockSpec(shape, lambda *_: (0,) * len(shape), pipeline_mode=pl.Buffered(1))


def _norm_proj_kernel(*refs, n_out, chunk):
    x_ref, nw_ref = refs[0], refs[1]
    w_refs = refs[2:2 + n_out]
    o_refs = refs[2 + n_out:]
    h = _rms(x_ref[...].astype(F32), nw_ref[...]).astype(BF16)
    for w_ref, o_ref in zip(w_refs, o_refs):
        for c0, c1 in _col_chunks(w_ref.shape[1], chunk):
            o_ref[:, c0:c1] = _dot(h, w_ref[:, c0:c1]).astype(o_ref.dtype)


def norm_proj(x, norm_w, weights, out_dtypes, *, tm=512, chunk=512):
    t, k = x.shape
    n_out = len(weights)
    in_specs = [pl.BlockSpec((tm, k), lambda i: (i, 0)), _const_spec((1, k))]
    in_specs += [_const_spec(w.shape) for w in weights]
    out_specs = [pl.BlockSpec((tm, w.shape[1]), lambda i: (i, 0)) for w in weights]
    out_shape = [jax.ShapeDtypeStruct((t, w.shape[1]), dt) for w, dt in zip(weights, out_dtypes)]
    return pl.pallas_call(
        functools.partial(_norm_proj_kernel, n_out=n_out, chunk=chunk),
        grid=(t // tm,),
        in_specs=in_specs,
        out_specs=out_specs,
        out_shape=out_shape,
        compiler_params=_cparams("parallel"),
        name="norm_proj",
    )(x, norm_w.reshape(1, k), *weights)


def _mix_ffn_kernel(*refs, n_in, chunk):
    x_ref = refs[0]
    a_refs = refs[1:1 + n_in]
    w_refs = refs[1 + n_in:1 + 2 * n_in]
    nw_ref, wg_ref, wu_ref, wd_ref, o_ref = refs[1 + 2 * n_in:]
    x = x_ref[...]
    for a_ref, w_ref in zip(a_refs, w_refs):
        x = x + _dot(a_ref[...], w_ref[...])
    o_ref[...] = x
    x = o_ref[...]
    h = _rms(x, nw_ref[...]).astype(BF16)
    acc = x
    for c0, c1 in _col_chunks(wg_ref.shape[1], chunk):
        g = _dot(h, wg_ref[:, c0:c1])
        u = _dot(h, wu_ref[:, c0:c1])
        a = (_silu(g) * u).astype(BF16)
        acc = acc + _dot(a, wd_ref[c0:c1, :])
    o_ref[...] = acc


def mix_ffn_residual(x, acts, weights, norm_w, wg, wu, wd, *, tm=512, chunk=256):
    t, d = x.shape
    n_in = len(acts)
    row = pl.BlockSpec((tm, d), lambda i: (i, 0))
    in_specs = [row]
    in_specs += [pl.BlockSpec((tm, a.shape[1]), lambda i: (i, 0)) for a in acts]
    in_specs += [_const_spec(w.shape) for w in weights]
    in_specs += [_const_spec((1, d)), _const_spec(wg.shape), _const_spec(wu.shape), _const_spec(wd.shape)]
    return pl.pallas_call(
        functools.partial(_mix_ffn_kernel, n_in=n_in, chunk=chunk),
        grid=(t // tm,),
        in_specs=in_specs,
        out_specs=row,
        out_shape=jax.ShapeDtypeStruct((t, d), F32),
        compiler_params=_cparams("parallel"),
        name="mix_ffn_residual",
    )(x, *acts, *weights, norm_w.reshape(1, d), wg, wu, wd)


_COL_Z = 0
_COL_V = _COL_Z + D_SSM
_COL_G = _COL_V + GLA_VAL_DIM
_COL_XBC = _COL_G + GLA_VAL_DIM
_COL_Q = _COL_XBC + CONV_DIM
_COL_K = _COL_Q + GLA_KEY_DIM
_COL_GKL = _COL_K + GLA_KEY_DIM
_PROJ_COLS = _COL_GKL + LANES
_CONV_HALO = 8


def _ssd_kernel(z_ref, xs_ref, bc_ref, dt_ref, cw_ref, cb_ref, dtb_ref, nega_ref, dskip_ref, nw_ref,
                tril_ref, exp_ref, rep_ref, o_ref, xpad_ref, state_ref):
    q = SSD_CHUNK
    p = SSD_HEAD_DIM
    n = SSD_STATE
    gw = SSD_HPG * p

    @pl.when(pl.program_id(1) == 0)
    def _():
        xpad_ref[0:_CONV_HALO, :] = jnp.zeros((_CONV_HALO, CONV_DIM), F32)
        state_ref[...] = jnp.zeros_like(state_ref)

    xpad_ref[_CONV_HALO:, 0:D_SSM] = xs_ref[...].astype(F32)
    xpad_ref[_CONV_HALO:, D_SSM:] = bc_ref[...].astype(F32)
    conv = cb_ref[...]
    for j in range(CONV_WIDTH):
        off = _CONV_HALO - (CONV_WIDTH - 1) + j
        conv = conv + cw_ref[j:j + 1, :] * xpad_ref[off:off + q, :]
    xpad_ref[0:_CONV_HALO, :] = xpad_ref[q:q + _CONV_HALO, :]
    xbc = _silu(conv)

    dt = _softplus(dt_ref[...] + dtb_ref[...])
    a = dt * nega_ref[...]
    acum = _sel_dot(tril_ref[...], a)
    acum_t = acum.T
    dt_x = _dot_sel(dt, exp_ref[...])
    acum_x = _dot_sel(acum, exp_ref[...])
    acum_rep = _dot_sel(acum, rep_ref[...])
    a_last_x = acum_x[q - 1:q, :]
    xs = xbc[:, 0:D_SSM]
    xdt = xs * dt_x
    xdt_bf = xdt.astype(BF16)
    w_end = (xdt * jnp.exp(a_last_x - acum_x)).astype(BF16)
    exp_a = jnp.exp(acum_x)
    chunk_decay = jnp.exp(a_last_x)
    y_skip = xs * dskip_ref[...]
    row = lax.broadcasted_iota(jnp.int32, (q, q), 0)
    col = lax.broadcasted_iota(jnp.int32, (q, q), 1)
    causal = row >= col
    lane = lax.broadcasted_iota(jnp.int32, (q, 2 * p), 1)

    y_groups = []
    for g in range(SSD_GROUPS):
        gs = slice(g * gw, (g + 1) * gw)
        b_g = xbc[:, D_SSM + g * n:D_SSM + (g + 1) * n]
        c_g = xbc[:, D_SSM + (SSD_GROUPS + g) * n:D_SSM + (SSD_GROUPS + g + 1) * n]
        b_bf = b_g.astype(BF16)
        c_bf = c_g.astype(BF16)
        cb = _dot_nt(c_bf, b_bf)
        s_g = state_ref[g]
        y_off = _dot(c_bf, s_g.astype(BF16)) * exp_a[:, gs]
        y_pairs = []
        for k in range(0, SSD_HPG, 2):
            lhs = []
            for hk in (g * SSD_HPG + k, g * SSD_HPG + k + 1):
                a_col = acum_rep[:, hk * q:(hk + 1) * q]
                a_row = acum_t[hk:hk + 1, :]
                decay = jnp.where(causal, jnp.exp(jnp.minimum(a_col - a_row, 0.0)), 0.0)
                lhs.append((cb * decay).astype(BF16))
            hk0 = g * SSD_HPG + k
            x_pair = xdt_bf[:, hk0 * p:(hk0 + 2) * p]
            zero = jnp.zeros_like(x_pair)
            rhs = jnp.concatenate([jnp.where(lane < p, x_pair, zero), jnp.where(lane >= p, x_pair, zero)], axis=0)
            y_pairs.append(_dot(jnp.concatenate(lhs, axis=1), rhs))
        state_ref[g] = s_g * chunk_decay[:, gs] + _dot_tn(b_bf, w_end[:, gs])
        y_g = jnp.concatenate(y_pairs, axis=1) + y_off + y_skip[:, gs]
        y_g = y_g * _silu(z_ref[:, gs].astype(F32))
        y_groups.append(_rms(y_g, nw_ref[:, gs]))
    o_ref[...] = jnp.concatenate(y_groups, axis=1).astype(o_ref.dtype)


def ssd_scan(proj, dt_raw, conv_w, conv_b, dt_bias, a_log, d_skip, norm_w, *, batch):
    t = proj.shape[0]
    q = SSD_CHUNK
    nc = t // batch // q

    def pad_heads(v):
        return jnp.pad(v.reshape(1, SSD_HEADS), ((0, 0), (0, LANES - SSD_HEADS)))

    tril = jnp.tril(jnp.ones((q, q), F32)).astype(BF16)
    head = jnp.arange(LANES)[:, None]
    expand = (head == jnp.arange(D_SSM)[None, :] // SSD_HEAD_DIM).astype(BF16)
    replicate = (head == jnp.arange(SSD_HEADS * q)[None, :] // q).astype(BF16)
    rows = lambda b, c: b * nc + c
    return pl.pallas_call(
        _ssd_kernel,
        grid=(batch, nc),
        in_specs=[
            pl.BlockSpec((q, D_SSM), lambda b, c: (rows(b, c), _COL_Z // D_SSM)),
            pl.BlockSpec((q, D_SSM), lambda b, c: (rows(b, c), _COL_XBC // D_SSM)),
            pl.BlockSpec((q, CONV_DIM - D_SSM), lambda b, c: (rows(b, c), (_COL_XBC + D_SSM) // (CONV_DIM - D_SSM))),
            pl.BlockSpec((q, LANES), lambda b, c: (rows(b, c), 0)),
            _const_spec((CONV_WIDTH, CONV_DIM)),
            _const_spec((1, CONV_DIM)),
            _const_spec((1, LANES)),
            _const_spec((1, LANES)),
            _const_spec((1, D_SSM)),
            _const_spec((1, D_SSM)),
            _const_spec((q, q)),
            _const_spec(expand.shape),
            _const_spec(replicate.shape),
        ],
        out_specs=pl.BlockSpec((q, D_SSM), lambda b, c: (rows(b, c), 0)),
        out_shape=jax.ShapeDtypeStruct((t, D_SSM), BF16),
        scratch_shapes=[
            pltpu.VMEM((q + _CONV_HALO, CONV_DIM), F32),
            pltpu.VMEM((SSD_GROUPS, SSD_STATE, SSD_HPG * SSD_HEAD_DIM), F32),
        ],
        compiler_params=_cparams("parallel", "arbitrary"),
        name="ssd_scan",
    )(proj, proj, proj, dt_raw, conv_w, conv_b.reshape(1, CONV_DIM), pad_heads(dt_bias),
      pad_heads(-jnp.exp(a_log)), jnp.repeat(d_skip, SSD_HEAD_DIM).reshape(1, D_SSM), norm_w.reshape(1, D_SSM),
      tril, expand, replicate)


def _gla_kernel(q_ref, k_ref, v_ref, g_ref, gkl_ref, w2_ref, gkb_ref, nw_ref, tril_ref, o_ref, state_ref,
                *, block):
    c = GLA_CHUNK
    dk, dv = GLA_HEAD_K, GLA_HEAD_V
    scale = dk ** -0.5

    @pl.when(pl.program_id(1) == 0)
    def _():
        state_ref[...] = jnp.zeros_like(state_ref)

    gk = -_softplus(-(_dot(gkl_ref[...], w2_ref[...]) + gkb_ref[...])) * (1.0 / GLA_GATE_NORM)
    row = lax.broadcasted_iota(jnp.int32, (c, c), 0)
    col = lax.broadcasted_iota(jnp.int32, (c, c), 1)
    causal = row >= col
    for j in range(block // c):
        r0, r1 = j * c, (j + 1) * c
        gcum_all = _sel_dot(tril_ref[...], gk[r0:r1])
        for h in range(GLA_HEADS):
            ks, vs = slice(h * dk, (h + 1) * dk), slice(h * dv, (h + 1) * dv)
            gcum = gcum_all[:, ks]
            g_last = gcum[c - 1:c, :]
            qc = q_ref[r0:r1, ks].astype(F32) * scale
            kc = k_ref[r0:r1, ks].astype(F32)
            vc = v_ref[r0:r1, vs]
            q_dec = (qc * jnp.exp(gcum)).astype(BF16)
            k_inv = (kc * jnp.exp(-gcum)).astype(BF16)
            k_end = (kc * jnp.exp(g_last - gcum)).astype(BF16)
            scores = jnp.where(causal, _dot_nt(q_dec, k_inv), 0.0)
            s_t = state_ref[h]
            o = _dot(scores.astype(BF16), vc) + _dot_nt(q_dec, s_t.astype(BF16))
            state_ref[h] = s_t * jnp.exp(g_last) + _dot_tn(vc, k_end)
            o = _rms(o, nw_ref[...]) * _silu(g_ref[r0:r1, vs].astype(F32))
            o_ref[r0:r1, vs] = o.astype(o_ref.dtype)


def gla_scan(proj, gk_w2, gk_b, norm_w, *, batch, block=256):
    t = proj.shape[0]
    nb = t // batch // block
    kd, vd = GLA_KEY_DIM, GLA_VAL_DIM
    w2 = jnp.pad(gk_w2, ((0, LANES - GLA_GATE_RANK), (0, 0))).astype(BF16)
    tril = jnp.tril(jnp.ones((GLA_CHUNK, GLA_CHUNK), F32)).astype(BF16)
    rows = lambda b, i: b * nb + i
    return pl.pallas_call(
        functools.partial(_gla_kernel, block=block),
        grid=(batch, nb),
        in_specs=[
            pl.BlockSpec((block, kd), lambda b, i: (rows(b, i), _COL_Q // kd)),
            pl.BlockSpec((block, kd), lambda b, i: (rows(b, i), _COL_K // kd)),
            pl.BlockSpec((block, vd), lambda b, i: (rows(b, i), _COL_V // vd)),
            pl.BlockSpec((block, vd), lambda b, i: (rows(b, i), _COL_G // vd)),
            pl.BlockSpec((block, LANES), lambda b, i: (rows(b, i), _COL_GKL // LANES)),
            _const_spec((LANES, kd)),
            _const_spec((1, kd)),
            _const_spec((1, GLA_HEAD_V)),
            _const_spec((GLA_CHUNK, GLA_CHUNK)),
        ],
        out_specs=pl.BlockSpec((block, vd), lambda b, i: (rows(b, i), 0)),
        out_shape=jax.ShapeDtypeStruct((t, vd), BF16),
        scratch_shapes=[pltpu.VMEM((GLA_HEADS, GLA_HEAD_V, GLA_HEAD_K), F32)],
        compiler_params=_cparams("parallel", "arbitrary"),
        name="gla_scan",
    )(proj, proj, proj, proj, proj, w2, gk_b.reshape(1, kd), norm_w.reshape(1, GLA_HEAD_V), tril)


_DQ_CQ = 0
_DQ_CKV = 512
_DQ_KRA = 768
_DQ_KRB = 896
_DQ_COLS = 1024
_QKV_COLS = MLA_HEADS * HEAD_PAD
V7X_MXU_COLS = 256
_HEAD_GROUP = V7X_MXU_COLS // HEAD_PAD
_GROUP_COLS = _HEAD_GROUP * HEAD_PAD


def _q_up_kernel(cq_ref, lw_ref, wa_ref, wb_ref, ga_ref, gb_ref, cos_ref, sin_ref, ind_ref, indt_ref, o_ref,
                 *, scale):
    c = _rms(cq_ref[...], lw_ref[...]).astype(BF16)
    fa = jnp.concatenate([ga_ref[...] * cos_ref[...] * scale] * _HEAD_GROUP, axis=1)
    fb = jnp.concatenate([gb_ref[...] * sin_ref[...] * scale] * _HEAD_GROUP, axis=1)
    for g in range(MLA_HEADS // _HEAD_GROUP):
        sl = slice(g * _GROUP_COLS, (g + 1) * _GROUP_COLS)
        a = _dot(c, wa_ref[:, sl])
        b = _dot(c, wb_ref[:, sl])
        ms = _dot((a * a).astype(BF16), ind_ref[...])
        r = _dot_sel(lax.rsqrt(ms + NORM_EPS), indt_ref[...])
        o_ref[:, sl] = ((a * fa + b * fb) * r).astype(o_ref.dtype)


def _kv_up_kernel(ckv_ref, kra_ref, krb_ref, lw_ref, wk_ref, wv_ref, gk_ref, gra_ref, grb_ref, cos_ref, sin_ref,
                  ind_ref, indt_ref, ones_ref, k_ref, v_ref):
    c = _rms(ckv_ref[...], lw_ref[...]).astype(BF16)
    kra = kra_ref[...]
    krb = krb_ref[...]
    r_rope = lax.rsqrt(jnp.sum(kra * kra, axis=-1, keepdims=True) * (1.0 / MLA_ROPE) + NORM_EPS)
    k_rope = (kra * (gra_ref[...] * cos_ref[...]) + krb * (grb_ref[...] * sin_ref[...])) * r_rope
    k_rope = jnp.concatenate([k_rope] * _HEAD_GROUP, axis=1)
    gk = jnp.concatenate([gk_ref[...]] * _HEAD_GROUP, axis=1)
    ones = jnp.concatenate([ones_ref[...]] * _HEAD_GROUP, axis=1)
    for g in range(MLA_HEADS // _HEAD_GROUP):
        sl = slice(g * _GROUP_COLS, (g + 1) * _GROUP_COLS)
        kn = _dot(c, wk_ref[:, sl])
        ms = _dot((kn * kn).astype(BF16), ind_ref[...])
        r = _dot_sel(lax.rsqrt(ms + NORM_EPS), indt_ref[...])
        k_ref[:, sl] = (kn * gk * r + k_rope).astype(k_ref.dtype)
        v_ref[:, sl] = (_dot(c, wv_ref[:, sl]) + ones).astype(v_ref.dtype)


def _head_selectors():
    lane = jnp.arange(_GROUP_COLS)
    head, pos = lane // HEAD_PAD, lane % HEAD_PAD
    nope = pos < MLA_NOPE
    rope = (pos >= MLA_NOPE) & (pos < MLA_NOPE + MLA_ROPE)
    col = jnp.arange(LANES)[None, :]
    sel_nope = nope[:, None] & (col == 2 * head[:, None])
    sel_rope = rope[:, None] & (col == 2 * head[:, None] + 1)
    ind = sel_nope.astype(F32) / MLA_NOPE + sel_rope.astype(F32) / MLA_ROPE
    indt = (sel_nope | sel_rope).astype(F32).T
    return ind.astype(BF16), indt.astype(BF16)


def _rope_tables(seqlen, batch):
    pos = jnp.arange(seqlen, dtype=F32)
    inv = 1.0 / (ROPE_THETA ** (jnp.arange(0, MLA_ROPE, 2, dtype=F32) / MLA_ROPE))
    ang = pos[:, None] * inv[None, :]
    cos, sin = jnp.cos(ang), jnp.sin(ang)
    pad = jnp.zeros((seqlen, HEAD_PAD - MLA_NOPE - MLA_ROPE), F32)
    cos_t = jnp.concatenate([jnp.ones((seqlen, MLA_NOPE), F32), cos, cos, pad], axis=1)
    sin_t = jnp.concatenate([jnp.zeros((seqlen, MLA_NOPE), F32), -sin, sin, pad], axis=1)
    return jnp.tile(cos_t, (batch, 1)), jnp.tile(sin_t, (batch, 1))


def _swap_halves(w):
    half = w.shape[-1] // 2
    return jnp.concatenate([w[..., half:], w[..., :half]], axis=-1)


def _rope_lanes(w):
    pad = [(0, 0)] * (w.ndim - 1) + [(MLA_NOPE, HEAD_PAD - MLA_NOPE - MLA_ROPE)]
    return jnp.pad(w, pad)


def q_up(cqkv, lora_norm, w_uq, nope_norm, rope_norm, cos_t, sin_t, *, tm=512):
    t = cqkv.shape[0]
    w = w_uq.reshape(MLA_Q_RANK, MLA_HEADS, MLA_NOPE + MLA_ROPE)
    w_nope, w_rope = w[..., :MLA_NOPE], w[..., MLA_NOPE:]
    pad = jnp.zeros((MLA_Q_RANK, MLA_HEADS, HEAD_PAD - MLA_NOPE - MLA_ROPE), F32)
    wa = jnp.concatenate([w_nope, w_rope, pad], axis=-1).reshape(MLA_Q_RANK, _QKV_COLS).astype(BF16)
    wb = _rope_lanes(_swap_halves(w_rope)).reshape(MLA_Q_RANK, _QKV_COLS).astype(BF16)
    ga = jnp.concatenate([nope_norm, rope_norm, jnp.zeros((HEAD_PAD - MLA_NOPE - MLA_ROPE,), F32)]).reshape(1, HEAD_PAD)
    gb = _rope_lanes(_swap_halves(rope_norm)).reshape(1, HEAD_PAD)
    ind, indt = _head_selectors()
    row = lambda width, j: pl.BlockSpec((tm, width), lambda i: (i, j))
    return pl.pallas_call(
        functools.partial(_q_up_kernel, scale=(MLA_NOPE + MLA_ROPE) ** -0.5 * math.log2(math.e)),
        grid=(t // tm,),
        in_specs=[
            row(MLA_Q_RANK, _DQ_CQ // MLA_Q_RANK),
            _const_spec((1, MLA_Q_RANK)),
            _const_spec(wa.shape), _const_spec(wb.shape),
            _const_spec((1, HEAD_PAD)), _const_spec((1, HEAD_PAD)),
            row(HEAD_PAD, 0), row(HEAD_PAD, 0),
            _const_spec(ind.shape), _const_spec(indt.shape),
        ],
        out_specs=row(_QKV_COLS, 0),
        out_shape=jax.ShapeDtypeStruct((t, _QKV_COLS), BF16),
        compiler_params=_cparams("parallel"),
        name="mla_q_up",
    )(cqkv, lora_norm.reshape(1, MLA_Q_RANK), wa, wb, ga, gb, cos_t, sin_t, ind, indt)


def kv_up(cqkv, lora_norm, w_ukv, nope_norm, rope_norm, cos_t, sin_t, *, tm=512):
    t = cqkv.shape[0]
    w = w_ukv.reshape(MLA_KV_RANK, MLA_HEADS, MLA_NOPE + MLA_V)
    pad_k = [(0, 0), (0, 0), (0, HEAD_PAD - MLA_NOPE)]
    pad_v = [(0, 0), (0, 0), (0, HEAD_PAD - MLA_V)]
    wk = jnp.pad(w[..., :MLA_NOPE], pad_k).reshape(MLA_KV_RANK, _QKV_COLS).astype(BF16)
    wv = jnp.pad(w[..., MLA_NOPE:], pad_v).reshape(MLA_KV_RANK, _QKV_COLS).astype(BF16)
    gk = jnp.pad(nope_norm, (0, HEAD_PAD - MLA_NOPE)).reshape(1, HEAD_PAD)
    gra = _rope_lanes(rope_norm).reshape(1, HEAD_PAD)
    grb = _rope_lanes(_swap_halves(rope_norm)).reshape(1, HEAD_PAD)
    ones = (jnp.arange(HEAD_PAD) >= MLA_V).astype(F32).reshape(1, HEAD_PAD)
    ind, indt = _head_selectors()
    row = lambda width, j: pl.BlockSpec((tm, width), lambda i: (i, j))
    return pl.pallas_call(
        _kv_up_kernel,
        grid=(t // tm,),
        in_specs=[
            row(MLA_KV_RANK, _DQ_CKV // MLA_KV_RANK),
            row(HEAD_PAD, _DQ_KRA // HEAD_PAD), row(HEAD_PAD, _DQ_KRB // HEAD_PAD),
            _const_spec((1, MLA_KV_RANK)),
            _const_spec(wk.shape), _const_spec(wv.shape),
            _const_spec((1, HEAD_PAD)), _const_spec((1, HEAD_PAD)), _const_spec((1, HEAD_PAD)),
            row(HEAD_PAD, 0), row(HEAD_PAD, 0),
            _const_spec(ind.shape), _const_spec(indt.shape), _const_spec((1, HEAD_PAD)),
        ],
        out_specs=[row(_QKV_COLS, 0), row(_QKV_COLS, 0)],
        out_shape=[jax.ShapeDtypeStruct((t, _QKV_COLS), BF16)] * 2,
        compiler_params=_cparams("parallel"),
        name="mla_kv_up",
    )(cqkv, cqkv, cqkv, lora_norm.reshape(1, MLA_KV_RANK), wk, wv, gk, gra, grb, cos_t, sin_t, ind, indt, ones)


def _attn_kernel(q_ref, k_ref, v_ref, o_ref, s_ref, p_ref, mx_ref, m_ref, alpha_ref, acc_ref, *, tq, tk):
    qi = pl.program_id(2)

    def rows(j):
        return pl.ds(pl.multiple_of(j * tk, tk), tk)

    def stage_a(slot, j, limit=None, r0=0):
        nr = tq - r0
        s = _dot_nt(q_ref[r0:, :], k_ref[rows(j), :])
        if limit is not None:
            rel = (lax.broadcasted_iota(jnp.int32, (nr, tk), 1) - lax.broadcasted_iota(jnp.int32, (nr, tk), 0))
            s = jnp.where(rel <= limit + r0, s, -jnp.inf)
        s_ref[slot, r0:, :] = s
        mx_ref[slot, r0:, :] = jnp.broadcast_to(jnp.max(s, axis=-1, keepdims=True), (nr, LANES))

    def stage_b(slot, r0=0):
        m = m_ref[r0:, :]
        m_new = jnp.maximum(m, mx_ref[slot, r0:, :])
        for c in range(0, tk, LANES):
            pb = jnp.exp2(s_ref[slot, r0:, c:c + LANES] - m_new).astype(BF16)
            p_ref[slot, r0 // 2:, c:c + LANES] = pltpu.bitcast(pb, jnp.uint32)
        m_ref[r0:, :] = m_new
        alpha_ref[r0:, :] = jnp.exp2(m - m_new)

    def stage_c(slot, j, r0=0):
        p = pltpu.bitcast(p_ref[slot, r0 // 2:, :], BF16)
        acc_ref[r0:, :] = alpha_ref[r0:, :] * acc_ref[r0:, :] + _dot(p, v_ref[rows(j), :])

    def trip(ii, limit):
        j = 2 * ii
        stage_a(1, j + 1)
        stage_c(1, jnp.maximum(j - 1, 0))
        stage_b(0)
        stage_a(0, j + 2, limit=limit)
        stage_c(0, j)
        stage_b(1)

    p_ref[1] = jnp.zeros((tq // 2, tk), jnp.uint32)
    m_ref[...] = jnp.full((tq, LANES), -jnp.inf, F32)
    alpha_ref[...] = jnp.ones((tq, LANES), F32)
    acc_ref[...] = jnp.zeros((tq, HEAD_PAD), F32)
    stage_a(0, 0, limit=qi * tq)

    @pl.loop(0, jnp.right_shift(qi - 1, 1))
    def _(i2):
        trip(2 * i2, None)
        trip(2 * i2 + 1, None)

    @pl.when(jnp.logical_and(qi >= 2, qi % 2 == 0))
    def _():
        trip(qi - 2, None)

    @pl.when(qi > 0)
    def _():
        trip(qi - 1, 0)

    jd = 2 * qi
    stage_a(1, jd + 1, limit=-tk, r0=tk)
    stage_c(1, jnp.maximum(jd - 1, 0))
    stage_b(0)
    stage_c(0, jd)
    stage_b(1, r0=tk)
    stage_c(1, jd + 1, r0=tk)
    acc = acc_ref[...]
    o_ref[...] = (acc / acc[:, MLA_V:MLA_V + 1]).astype(o_ref.dtype)


def flash_attention(q, k, v, *, batch, tk=512):
    t = q.shape[0]
    seqlen = t // batch
    tq = 2 * tk
    nq = seqlen // tq
    return pl.pallas_call(
        functools.partial(_attn_kernel, tq=tq, tk=tk),
        grid=(batch, MLA_HEADS, nq),
        in_specs=[
            pl.BlockSpec((tq, HEAD_PAD), lambda b, h, i: (b * nq + i, h)),
            pl.BlockSpec((seqlen, HEAD_PAD), lambda b, h, i: (b, h)),
            pl.BlockSpec((seqlen, HEAD_PAD), lambda b, h, i: (b, h)),
        ],
        out_specs=pl.BlockSpec((tq, HEAD_PAD), lambda b, h, i: (b * nq + i, h)),
        out_shape=jax.ShapeDtypeStruct((t, _QKV_COLS), BF16),
        scratch_shapes=[
            pltpu.VMEM((2, tq, tk), F32),
            pltpu.VMEM((2, tq // 2, tk), jnp.uint32),
            pltpu.VMEM((2, tq, LANES), F32),
            pltpu.VMEM((tq, LANES), F32),
            pltpu.VMEM((tq, LANES), F32),
            pltpu.VMEM((tq, HEAD_PAD), F32),
        ],
        compiler_params=_cparams("parallel", "parallel", "arbitrary"),
        name="mla_attention",
    )(q, k, v)


def _even_mixer(x, batch, norm_w, w_in, conv_w, conv_b, dt_bias, a_log, d_skip, ssd_norm,
                gk_w2, gk_b, gla_norm, w_out):
    sizes = (D_SSM, CONV_DIM, SSD_HEADS, GLA_KEY_DIM, GLA_KEY_DIM, GLA_VAL_DIM, GLA_VAL_DIM, GLA_GATE_RANK)
    offs = [0]
    for s in sizes:
        offs.append(offs[-1] + s)
    seg = [w_in[:, offs[i]:offs[i + 1]] for i in range(len(sizes))]
    w_z, w_xbc, w_dt, w_q, w_k, w_v, w_g, w_gkl = seg
    w_main = jnp.concatenate(
        [w_z, w_v, w_g, w_xbc, w_q, w_k, jnp.pad(w_gkl, ((0, 0), (0, LANES - GLA_GATE_RANK)))], axis=1).astype(BF16)
    w_dtp = jnp.pad(w_dt, ((0, 0), (0, LANES - SSD_HEADS))).astype(BF16)
    proj, dt_raw = norm_proj(x, norm_w, [w_main, w_dtp], [BF16, F32])
    y_ssd = ssd_scan(proj, dt_raw, conv_w, conv_b, dt_bias, a_log, d_skip, ssd_norm, batch=batch)
    y_gla = gla_scan(proj, gk_w2, gk_b, gla_norm, batch=batch)
    w_o = w_out.astype(BF16)
    return [y_ssd, y_gla], [w_o[:D_SSM], w_o[D_SSM:]]


def _odd_mixer(x, batch, cos_t, sin_t, norm_w, w_dqkv, q_lora_norm, w_uq, kv_lora_norm, w_ukv,
               q_nope_norm, q_rope_norm, k_nope_norm, k_rope_norm, w_o):
    w_cq = w_dqkv[:, :MLA_Q_RANK]
    w_ckv = w_dqkv[:, MLA_Q_RANK:MLA_Q_RANK + MLA_KV_RANK]
    w_kr = w_dqkv[:, MLA_Q_RANK + MLA_KV_RANK:]
    w_d = jnp.concatenate(
        [w_cq, jnp.zeros((D_MODEL, _DQ_CKV - MLA_Q_RANK), F32), w_ckv, _rope_lanes(w_kr),
         _rope_lanes(_swap_halves(w_kr))], axis=1).astype(BF16)
    (cqkv,) = norm_proj(x, norm_w, [w_d], [F32])
    q = q_up(cqkv, q_lora_norm, w_uq, q_nope_norm, q_rope_norm, cos_t, sin_t)
    k, v = kv_up(cqkv, kv_lora_norm, w_ukv, k_nope_norm, k_rope_norm, cos_t, sin_t)
    o = flash_attention(q, k, v, batch=batch)
    w_oe = jnp.pad(w_o.reshape(MLA_HEADS, MLA_V, D_MODEL), ((0, 0), (0, HEAD_PAD - MLA_V), (0, 0)))
    return [o], [w_oe.reshape(_QKV_COLS, D_MODEL).astype(BF16)]


def kernel(x, mix_norm_even, w_in_even, conv_w, conv_b, dt_bias, a_log, d_skip, ssd_norm, gla_gk_w2, gla_gk_b, gla_norm, w_out_even, mix_norm_odd, w_dqkv, q_lora_norm, w_uq, kv_lora_norm, w_ukv, q_nope_norm, q_rope_norm, k_nope_norm, k_rope_norm, w_o_mla, ffn_norm, w_gate, w_up, w_down):
    batch, seqlen, d = x.shape
    depth = ffn_norm.shape[0]
    cos_t, sin_t = _rope_tables(seqlen, batch)
    x = x.reshape(batch * seqlen, d)
    for i in range(depth):
        j = i // 2
        if i % 2 == 0:
            acts, w_outs = _even_mixer(x, batch, mix_norm_even[j], w_in_even[j], conv_w[j], conv_b[j], dt_bias[j],
                                       a_log[j], d_skip[j], ssd_norm[j], gla_gk_w2[j], gla_gk_b[j], gla_norm[j],
                                       w_out_even[j])
        else:
            acts, w_outs = _odd_mixer(x, batch, cos_t, sin_t, mix_norm_odd[j], w_dqkv[j], q_lora_norm[j], w_uq[j],
                                      kv_lora_norm[j], w_ukv[j], q_nope_norm[j], q_rope_norm[j], k_nope_norm[j],
                                      k_rope_norm[j], w_o_mla[j])
        x = mix_ffn_residual(x, acts, w_outs, ffn_norm[i], w_gate[i].astype(BF16), w_up[i].astype(BF16),
                             w_down[i].astype(BF16))
    return x.reshape(batch, seqlen, d)
```

```python
import functools
import math

import jax
import jax.numpy as jnp
from jax import lax
from jax.experimental import pallas as pl
from jax.experimental.pallas import tpu as pltpu

F32 = jnp.float32
BF16 = jnp.bfloat16

D_MODEL = 1024
NORM_EPS = 1e-6

SSD_HEAD_DIM = 64
SSD_HEADS = 16
SSD_GROUPS = 2
SSD_HPG = SSD_HEADS // SSD_GROUPS
SSD_STATE = 128
SSD_CHUNK = 128
CONV_WIDTH = 4
D_SSM = SSD_HEADS * SSD_HEAD_DIM
CONV_DIM = D_SSM + 2 * SSD_GROUPS * SSD_STATE

GLA_HEADS = 4
GLA_KEY_DIM = D_MODEL // 2
GLA_VAL_DIM = D_MODEL
GLA_HEAD_K = GLA_KEY_DIM // GLA_HEADS
GLA_HEAD_V = GLA_VAL_DIM // GLA_HEADS
GLA_GATE_RANK = 16
GLA_GATE_NORM = 16.0
GLA_CHUNK = 64

MLA_HEADS = 16
MLA_NOPE = 64
MLA_ROPE = 32
MLA_V = 64
MLA_Q_RANK = 384
MLA_KV_RANK = 256
ROPE_THETA = 10000.0

LANES = 128
HEAD_PAD = 128
V7X_VMEM_BYTES = 64 * 1024 * 1024
VMEM_LIMIT = V7X_VMEM_BYTES - 8 * 1024 * 1024


def _cparams(*sem):
    return pltpu.CompilerParams(dimension_semantics=sem, vmem_limit_bytes=VMEM_LIMIT)


def _dot(a, b):
    return jnp.dot(a, b, preferred_element_type=F32)


def _dot_nt(a, b):
    return lax.dot_general(a, b, (((1,), (1,)), ((), ())), preferred_element_type=F32)


def _dot_tn(a, b):
    return lax.dot_general(a, b, (((0,), (0,)), ((), ())), preferred_element_type=F32)


def _split3(x):
    hi = x.astype(BF16)
    r1 = x - hi.astype(F32)
    mid = r1.astype(BF16)
    lo = (r1 - mid.astype(F32)).astype(BF16)
    return hi, mid, lo


def _dot_sel(x, m):
    hi, mid, lo = _split3(x)
    return _dot(hi, m) + _dot(mid, m) + _dot(lo, m)


def _sel_dot(m, x):
    hi, mid, lo = _split3(x)
    return _dot(m, hi) + _dot(m, mid) + _dot(m, lo)


def _rms(x, w):
    return x * lax.rsqrt(jnp.mean(x * x, axis=-1, keepdims=True) + NORM_EPS) * w


def _silu(x):
    return x * jax.nn.sigmoid(x)


def _softplus(x):
    return jnp.maximum(x, 0.0) + jnp.log1p(jnp.exp(-jnp.abs(x)))


def _col_chunks(n, step):
    return [(c, min(c + step, n)) for c in range(0, n, step)]


def _const_spec(shape):
    return pl.BlockSpec(shape, lambda *_: (0,) * len(shape), pipeline_mode=pl.Buffered(1))


def _norm_proj_kernel(*refs, n_out, chunk):
    x_ref, nw_ref = refs[0], refs[1]
    w_refs = refs[2:2 + n_out]
    o_refs = refs[2 + n_out:]
    h = _rms(x_ref[...].astype(F32), nw_ref[...]).astype(BF16)
    for w_ref, o_ref in zip(w_refs, o_refs):
        for c0, c1 in _col_chunks(w_ref.shape[1], chunk):
            o_ref[:, c0:c1] = _dot(h, w_ref[:, c0:c1]).astype(o_ref.dtype)


def norm_proj(x, norm_w, weights, out_dtypes, *, tm=512, chunk=512):
    t, k = x.shape
    n_out = len(weights)
    in_specs = [pl.BlockSpec((tm, k), lambda i: (i, 0)), _const_spec((1, k))]
    in_specs += [_const_spec(w.shape) for w in weights]
    out_specs = [pl.BlockSpec((tm, w.shape[1]), lambda i: (i, 0)) for w in weights]
    out_shape = [jax.ShapeDtypeStruct((t, w.shape[1]), dt) for w, dt in zip(weights, out_dtypes)]
    return pl.pallas_call(
        functools.partial(_norm_proj_kernel, n_out=n_out, chunk=chunk),
        grid=(t // tm,),
        in_specs=in_specs,
        out_specs=out_specs,
        out_shape=out_shape,
        compiler_params=_cparams("parallel"),
        name="norm_proj",
    )(x, norm_w.reshape(1, k), *weights)


def _mix_ffn_kernel(*refs, n_in, chunk):
    x_ref = refs[0]
    a_refs = refs[1:1 + n_in]
    w_refs = refs[1 + n_in:1 + 2 * n_in]
    nw_ref, wg_ref, wu_ref, wd_ref, o_ref = refs[1 + 2 * n_in:]
    x = x_ref[...]
    for a_ref, w_ref in zip(a_refs, w_refs):
        x = x + _dot(a_ref[...], w_ref[...])
    o_ref[...] = x
    x = o_ref[...]
    h = _rms(x, nw_ref[...]).astype(BF16)
    acc = x
    for c0, c1 in _col_chunks(wg_ref.shape[1], chunk):
        g = _dot(h, wg_ref[:, c0:c1])
        u = _dot(h, wu_ref[:, c0:c1])
        a = (_silu(g) * u).astype(BF16)
        acc = acc + _dot(a, wd_ref[c0:c1, :])
    o_ref[...] = acc


def mix_ffn_residual(x, acts, weights, norm_w, wg, wu, wd, *, tm=512, chunk=256):
    t, d = x.shape
    n_in = len(acts)
    row = pl.BlockSpec((tm, d), lambda i: (i, 0))
    in_specs = [row]
    in_specs += [pl.BlockSpec((tm, a.shape[1]), lambda i: (i, 0)) for a in acts]
    in_specs += [_const_spec(w.shape) for w in weights]
    in_specs += [_const_spec((1, d)), _const_spec(wg.shape), _const_spec(wu.shape), _const_spec(wd.shape)]
    return pl.pallas_call(
        functools.partial(_mix_ffn_kernel, n_in=n_in, chunk=chunk),
        grid=(t // tm,),
        in_specs=in_specs,
        out_specs=row,
        out_shape=jax.ShapeDtypeStruct((t, d), F32),
        compiler_params=_cparams("parallel"),
        name="mix_ffn_residual",
    )(x, *acts, *weights, norm_w.reshape(1, d), wg, wu, wd)


_COL_Z = 0
_COL_V = _COL_Z + D_SSM
_COL_G = _COL_V + GLA_VAL_DIM
_COL_XBC = _COL_G + GLA_VAL_DIM
_COL_Q = _COL_XBC + CONV_DIM
_COL_K = _COL_Q + GLA_KEY_DIM
_COL_GKL = _COL_K + GLA_KEY_DIM
_PROJ_COLS = _COL_GKL + LANES
_CONV_HALO = 8


def _ssd_kernel(z_ref, xs_ref, bc_ref, dt_ref, cw_ref, cb_ref, dtb_ref, nega_ref, dskip_ref, nw_ref,
                tril_ref, exp_ref, rep_ref, o_ref, xpad_ref, state_ref):
    q = SSD_CHUNK
    p = SSD_HEAD_DIM
    n = SSD_STATE
    gw = SSD_HPG * p

    @pl.when(pl.program_id(1) == 0)
    def _():
        xpad_ref[0:_CONV_HALO, :] = jnp.zeros((_CONV_HALO, CONV_DIM), F32)
        state_ref[...] = jnp.zeros_like(state_ref)

    xpad_ref[_CONV_HALO:, 0:D_SSM] = xs_ref[...].astype(F32)
    xpad_ref[_CONV_HALO:, D_SSM:] = bc_ref[...].astype(F32)
    conv = cb_ref[...]
    for j in range(CONV_WIDTH):
        off = _CONV_HALO - (CONV_WIDTH - 1) + j
        conv = conv + cw_ref[j:j + 1, :] * xpad_ref[off:off + q, :]
    xpad_ref[0:_CONV_HALO, :] = xpad_ref[q:q + _CONV_HALO, :]
    xbc = _silu(conv)

    dt = _softplus(dt_ref[...] + dtb_ref[...])
    a = dt * nega_ref[...]
    acum = _sel_dot(tril_ref[...], a)
    acum_t = acum.T
    dt_x = _dot_sel(dt, exp_ref[...])
    acum_x = _dot_sel(acum, exp_ref[...])
    acum_rep = _dot_sel(acum, rep_ref[...])
    a_last_x = acum_x[q - 1:q, :]
    xs = xbc[:, 0:D_SSM]
    xdt = xs * dt_x
    xdt_bf = xdt.astype(BF16)
    w_end = (xdt * jnp.exp(a_last_x - acum_x)).astype(BF16)
    exp_a = jnp.exp(acum_x)
    chunk_decay = jnp.exp(a_last_x)
    y_skip = xs * dskip_ref[...]
    row = lax.broadcasted_iota(jnp.int32, (q, q), 0)
    col = lax.broadcasted_iota(jnp.int32, (q, q), 1)
    causal = row >= col
    lane = lax.broadcasted_iota(jnp.int32, (q, 2 * p), 1)

    y_groups = []
    for g in range(SSD_GROUPS):
        gs = slice(g * gw, (g + 1) * gw)
        b_g = xbc[:, D_SSM + g * n:D_SSM + (g + 1) * n]
        c_g = xbc[:, D_SSM + (SSD_GROUPS + g) * n:D_SSM + (SSD_GROUPS + g + 1) * n]
        b_bf = b_g.astype(BF16)
        c_bf = c_g.astype(BF16)
        cb = _dot_nt(c_bf, b_bf)
        s_g = state_ref[g]
        y_off = _dot(c_bf, s_g.astype(BF16)) * exp_a[:, gs]
        y_pairs = []
        for k in range(0, SSD_HPG, 2):
            lhs = []
            for hk in (g * SSD_HPG + k, g * SSD_HPG + k + 1):
                a_col = acum_rep[:, hk * q:(hk + 1) * q]
                a_row = acum_t[hk:hk + 1, :]
                decay = jnp.where(causal, jnp.exp(jnp.minimum(a_col - a_row, 0.0)), 0.0)
                lhs.append((cb * decay).astype(BF16))
            hk0 = g * SSD_HPG + k
            x_pair = xdt_bf[:, hk0 * p:(hk0 + 2) * p]
            zero = jnp.zeros_like(x_pair)
            rhs = jnp.concatenate([jnp.where(lane < p, x_pair, zero), jnp.where(lane >= p, x_pair, zero)], axis=0)
            y_pairs.append(_dot(jnp.concatenate(lhs, axis=1), rhs))
        state_ref[g] = s_g * chunk_decay[:, gs] + _dot_tn(b_bf, w_end[:, gs])
        y_g = jnp.concatenate(y_pairs, axis=1) + y_off + y_skip[:, gs]
        y_g = y_g * _silu(z_ref[:, gs].astype(F32))
        y_groups.append(_rms(y_g, nw_ref[:, gs]))
    o_ref[...] = jnp.concatenate(y_groups, axis=1).astype(o_ref.dtype)


def ssd_scan(proj, dt_raw, conv_w, conv_b, dt_bias, a_log, d_skip, norm_w, *, batch):
    t = proj.shape[0]
    q = SSD_CHUNK
    nc = t // batch // q

    def pad_heads(v):
        return jnp.pad(v.reshape(1, SSD_HEADS), ((0, 0), (0, LANES - SSD_HEADS)))

    tril = jnp.tril(jnp.ones((q, q), F32)).astype(BF16)
    head = jnp.arange(LANES)[:, None]
    expand = (head == jnp.arange(D_SSM)[None, :] // SSD_HEAD_DIM).astype(BF16)
    replicate = (head == jnp.arange(SSD_HEADS * q)[None, :] // q).astype(BF16)
    rows = lambda b, c: b * nc + c
    return pl.pallas_call(
        _ssd_kernel,
        grid=(batch, nc),
        in_specs=[
            pl.BlockSpec((q, D_SSM), lambda b, c: (rows(b, c), _COL_Z // D_SSM)),
            pl.BlockSpec((q, D_SSM), lambda b, c: (rows(b, c), _COL_XBC // D_SSM)),
            pl.BlockSpec((q, CONV_DIM - D_SSM), lambda b, c: (rows(b, c), (_COL_XBC + D_SSM) // (CONV_DIM - D_SSM))),
            pl.BlockSpec((q, LANES), lambda b, c: (rows(b, c), 0)),
            _const_spec((CONV_WIDTH, CONV_DIM)),
            _const_spec((1, CONV_DIM)),
            _const_spec((1, LANES)),
            _const_spec((1, LANES)),
            _const_spec((1, D_SSM)),
            _const_spec((1, D_SSM)),
            _const_spec((q, q)),
            _const_spec(expand.shape),
            _const_spec(replicate.shape),
        ],
        out_specs=pl.BlockSpec((q, D_SSM), lambda b, c: (rows(b, c), 0)),
        out_shape=jax.ShapeDtypeStruct((t, D_SSM), BF16),
        scratch_shapes=[
            pltpu.VMEM((q + _CONV_HALO, CONV_DIM), F32),
            pltpu.VMEM((SSD_GROUPS, SSD_STATE, SSD_HPG * SSD_HEAD_DIM), F32),
        ],
        compiler_params=_cparams("parallel", "arbitrary"),
        name="ssd_scan",
    )(proj, proj, proj, dt_raw, conv_w, conv_b.reshape(1, CONV_DIM), pad_heads(dt_bias),
      pad_heads(-jnp.exp(a_log)), jnp.repeat(d_skip, SSD_HEAD_DIM).reshape(1, D_SSM), norm_w.reshape(1, D_SSM),
      tril, expand, replicate)


def _gla_kernel(q_ref, k_ref, v_ref, g_ref, gkl_ref, w2_ref, gkb_ref, nw_ref, tril_ref, o_ref, state_ref,
                *, block):
    c = GLA_CHUNK
    dk, dv = GLA_HEAD_K, GLA_HEAD_V
    scale = dk ** -0.5

    @pl.when(pl.program_id(1) == 0)
    def _():
        state_ref[...] = jnp.zeros_like(state_ref)

    gk = -_softplus(-(_dot(gkl_ref[...], w2_ref[...]) + gkb_ref[...])) * (1.0 / GLA_GATE_NORM)
    row = lax.broadcasted_iota(jnp.int32, (c, c), 0)
    col = lax.broadcasted_iota(jnp.int32, (c, c), 1)
    causal = row >= col
    for j in range(block // c):
        r0, r1 = j * c, (j + 1) * c
        gcum_all = _sel_dot(tril_ref[...], gk[r0:r1])
        for h in range(GLA_HEADS):
            ks, vs = slice(h * dk, (h + 1) * dk), slice(h * dv, (h + 1) * dv)
            gcum = gcum_all[:, ks]
            g_last = gcum[c - 1:c, :]
            qc = q_ref[r0:r1, ks].astype(F32) * scale
            kc = k_ref[r0:r1, ks].astype(F32)
            vc = v_ref[r0:r1, vs]
            q_dec = (qc * jnp.exp(gcum)).astype(BF16)
            k_inv = (kc * jnp.exp(-gcum)).astype(BF16)
            k_end = (kc * jnp.exp(g_last - gcum)).astype(BF16)
            scores = jnp.where(causal, _dot_nt(q_dec, k_inv), 0.0)
            s_t = state_ref[h]
            o = _dot(scores.astype(BF16), vc) + _dot_nt(q_dec, s_t.astype(BF16))
            state_ref[h] = s_t * jnp.exp(g_last) + _dot_tn(vc, k_end)
            o = _rms(o, nw_ref[...]) * _silu(g_ref[r0:r1, vs].astype(F32))
            o_ref[r0:r1, vs] = o.astype(o_ref.dtype)


def gla_scan(proj, gk_w2, gk_b, norm_w, *, batch, block=256):
    t = proj.shape[0]
    nb = t // batch // block
    kd, vd = GLA_KEY_DIM, GLA_VAL_DIM
    w2 = jnp.pad(gk_w2, ((0, LANES - GLA_GATE_RANK), (0, 0))).astype(BF16)
    tril = jnp.tril(jnp.ones((GLA_CHUNK, GLA_CHUNK), F32)).astype(BF16)
    rows = lambda b, i: b * nb + i
    return pl.pallas_call(
        functools.partial(_gla_kernel, block=block),
        grid=(batch, nb),
        in_specs=[
            pl.BlockSpec((block, kd), lambda b, i: (rows(b, i), _COL_Q // kd)),
            pl.BlockSpec((block, kd), lambda b, i: (rows(b, i), _COL_K // kd)),
            pl.BlockSpec((block, vd), lambda b, i: (rows(b, i), _COL_V // vd)),
            pl.BlockSpec((block, vd), lambda b, i: (rows(b, i), _COL_G // vd)),
            pl.BlockSpec((block, LANES), lambda b, i: (rows(b, i), _COL_GKL // LANES)),
            _const_spec((LANES, kd)),
            _const_spec((1, kd)),
            _const_spec((1, GLA_HEAD_V)),
            _const_spec((GLA_CHUNK, GLA_CHUNK)),
        ],
        out_specs=pl.BlockSpec((block, vd), lambda b, i: (rows(b, i), 0)),
        out_shape=jax.ShapeDtypeStruct((t, vd), BF16),
        scratch_shapes=[pltpu.VMEM((GLA_HEADS, GLA_HEAD_V, GLA_HEAD_K), F32)],
        compiler_params=_cparams("parallel", "arbitrary"),
        name="gla_scan",
    )(proj, proj, proj, proj, proj, w2, gk_b.reshape(1, kd), norm_w.reshape(1, GLA_HEAD_V), tril)


_DQ_CQ = 0
_DQ_CKV = 512
_DQ_KRA = 768
_DQ_KRB = 896
_DQ_COLS = 1024
_QKV_COLS = MLA_HEADS * HEAD_PAD
V7X_MXU_COLS = 256
_HEAD_GROUP = V7X_MXU_COLS // HEAD_PAD
_GROUP_COLS = _HEAD_GROUP * HEAD_PAD


def _q_up_kernel(cq_ref, lw_ref, wa_ref, wb_ref, ga_ref, gb_ref, cos_ref, sin_ref, pm_ref, o_ref, *, scale):
    c = _rms(cq_ref[...], lw_ref[...]).astype(BF16)
    fa = jnp.concatenate([ga_ref[...] * cos_ref[...] * scale] * _HEAD_GROUP, axis=1)
    fb = jnp.concatenate([gb_ref[...] * sin_ref[...] * scale] * _HEAD_GROUP, axis=1)
    for g in range(MLA_HEADS // _HEAD_GROUP):
        sl = slice(g * _GROUP_COLS, (g + 1) * _GROUP_COLS)
        a = _dot(c, wa_ref[:, sl])
        b = _dot(c, wb_ref[:, sl])
        ms = _dot((a * a).astype(BF16), pm_ref[...])
        o_ref[:, sl] = ((a * fa + b * fb) * lax.rsqrt(ms + NORM_EPS)).astype(o_ref.dtype)


def _kv_up_kernel(ckv_ref, kra_ref, krb_ref, lw_ref, wk_ref, wv_ref, gk_ref, gra_ref, grb_ref, cos_ref, sin_ref,
                  pm_ref, ones_ref, k_ref, v_ref):
    c = _rms(ckv_ref[...], lw_ref[...]).astype(BF16)
    kra = kra_ref[...]
    krb = krb_ref[...]
    r_rope = lax.rsqrt(jnp.sum(kra * kra, axis=-1, keepdims=True) * (1.0 / MLA_ROPE) + NORM_EPS)
    k_rope = (kra * (gra_ref[...] * cos_ref[...]) + krb * (grb_ref[...] * sin_ref[...])) * r_rope
    k_rope = jnp.concatenate([k_rope] * _HEAD_GROUP, axis=1)
    gk = jnp.concatenate([gk_ref[...]] * _HEAD_GROUP, axis=1)
    ones = jnp.concatenate([ones_ref[...]] * _HEAD_GROUP, axis=1)
    for g in range(MLA_HEADS // _HEAD_GROUP):
        sl = slice(g * _GROUP_COLS, (g + 1) * _GROUP_COLS)
        kn = _dot(c, wk_ref[:, sl])
        ms = _dot((kn * kn).astype(BF16), pm_ref[...])
        k_ref[:, sl] = (kn * gk * lax.rsqrt(ms + NORM_EPS) + k_rope).astype(k_ref.dtype)
        v_ref[:, sl] = (_dot(c, wv_ref[:, sl]) + ones).astype(v_ref.dtype)


def _part_mean_matrix():
    lane = jnp.arange(_GROUP_COLS)
    head, pos = lane // HEAD_PAD, lane % HEAD_PAD
    part = jnp.where(pos < MLA_NOPE, 0, jnp.where(pos < MLA_NOPE + MLA_ROPE, 1, 2))
    size = jnp.where(part == 0, MLA_NOPE, MLA_ROPE).astype(F32)
    same = (head[:, None] == head[None, :]) & (part[:, None] == part[None, :]) & (part[:, None] < 2)
    return (same.astype(F32) / size[None, :]).astype(BF16)


def _rope_tables(seqlen, batch):
    pos = jnp.arange(seqlen, dtype=F32)
    inv = 1.0 / (ROPE_THETA ** (jnp.arange(0, MLA_ROPE, 2, dtype=F32) / MLA_ROPE))
    ang = pos[:, None] * inv[None, :]
    cos, sin = jnp.cos(ang), jnp.sin(ang)
    pad = jnp.zeros((seqlen, HEAD_PAD - MLA_NOPE - MLA_ROPE), F32)
    cos_t = jnp.concatenate([jnp.ones((seqlen, MLA_NOPE), F32), cos, cos, pad], axis=1)
    sin_t = jnp.concatenate([jnp.zeros((seqlen, MLA_NOPE), F32), -sin, sin, pad], axis=1)
    return jnp.tile(cos_t, (batch, 1)), jnp.tile(sin_t, (batch, 1))


def _swap_halves(w):
    half = w.shape[-1] // 2
    return jnp.concatenate([w[..., half:], w[..., :half]], axis=-1)


def _rope_lanes(w):
    pad = [(0, 0)] * (w.ndim - 1) + [(MLA_NOPE, HEAD_PAD - MLA_NOPE - MLA_ROPE)]
    return jnp.pad(w, pad)


def q_up(cqkv, lora_norm, w_uq, nope_norm, rope_norm, cos_t, sin_t, *, tm=512):
    t = cqkv.shape[0]
    w = w_uq.reshape(MLA_Q_RANK, MLA_HEADS, MLA_NOPE + MLA_ROPE)
    w_nope, w_rope = w[..., :MLA_NOPE], w[..., MLA_NOPE:]
    pad = jnp.zeros((MLA_Q_RANK, MLA_HEADS, HEAD_PAD - MLA_NOPE - MLA_ROPE), F32)
    wa = jnp.concatenate([w_nope, w_rope, pad], axis=-1).reshape(MLA_Q_RANK, _QKV_COLS).astype(BF16)
    wb = _rope_lanes(_swap_halves(w_rope)).reshape(MLA_Q_RANK, _QKV_COLS).astype(BF16)
    ga = jnp.concatenate([nope_norm, rope_norm, jnp.zeros((HEAD_PAD - MLA_NOPE - MLA_ROPE,), F32)]).reshape(1, HEAD_PAD)
    gb = _rope_lanes(_swap_halves(rope_norm)).reshape(1, HEAD_PAD)
    pm = _part_mean_matrix()
    row = lambda width, j: pl.BlockSpec((tm, width), lambda i: (i, j))
    return pl.pallas_call(
        functools.partial(_q_up_kernel, scale=(MLA_NOPE + MLA_ROPE) ** -0.5 * math.log2(math.e)),
        grid=(t // tm,),
        in_specs=[
            row(MLA_Q_RANK, _DQ_CQ // MLA_Q_RANK),
            _const_spec((1, MLA_Q_RANK)),
            _const_spec(wa.shape), _const_spec(wb.shape),
            _const_spec((1, HEAD_PAD)), _const_spec((1, HEAD_PAD)),
            row(HEAD_PAD, 0), row(HEAD_PAD, 0),
            _const_spec(pm.shape),
        ],
        out_specs=row(_QKV_COLS, 0),
        out_shape=jax.ShapeDtypeStruct((t, _QKV_COLS), BF16),
        compiler_params=_cparams("parallel"),
        name="mla_q_up",
    )(cqkv, lora_norm.reshape(1, MLA_Q_RANK), wa, wb, ga, gb, cos_t, sin_t, pm)


def kv_up(cqkv, lora_norm, w_ukv, nope_norm, rope_norm, cos_t, sin_t, *, tm=512):
    t = cqkv.shape[0]
    w = w_ukv.reshape(MLA_KV_RANK, MLA_HEADS, MLA_NOPE + MLA_V)
    pad_k = [(0, 0), (0, 0), (0, HEAD_PAD - MLA_NOPE)]
    pad_v = [(0, 0), (0, 0), (0, HEAD_PAD - MLA_V)]
    wk = jnp.pad(w[..., :MLA_NOPE], pad_k).reshape(MLA_KV_RANK, _QKV_COLS).astype(BF16)
    wv = jnp.pad(w[..., MLA_NOPE:], pad_v).reshape(MLA_KV_RANK, _QKV_COLS).astype(BF16)
    gk = jnp.pad(nope_norm, (0, HEAD_PAD - MLA_NOPE)).reshape(1, HEAD_PAD)
    gra = _rope_lanes(rope_norm).reshape(1, HEAD_PAD)
    grb = _rope_lanes(_swap_halves(rope_norm)).reshape(1, HEAD_PAD)
    ones = (jnp.arange(HEAD_PAD) >= MLA_V).astype(F32).reshape(1, HEAD_PAD)
    pm = _part_mean_matrix()
    row = lambda width, j: pl.BlockSpec((tm, width), lambda i: (i, j))
    return pl.pallas_call(
        _kv_up_kernel,
        grid=(t // tm,),
        in_specs=[
            row(MLA_KV_RANK, _DQ_CKV // MLA_KV_RANK),
            row(HEAD_PAD, _DQ_KRA // HEAD_PAD), row(HEAD_PAD, _DQ_KRB // HEAD_PAD),
            _const_spec((1, MLA_KV_RANK)),
            _const_spec(wk.shape), _const_spec(wv.shape),
            _const_spec((1, HEAD_PAD)), _const_spec((1, HEAD_PAD)), _const_spec((1, HEAD_PAD)),
            row(HEAD_PAD, 0), row(HEAD_PAD, 0),
            _const_spec(pm.shape), _const_spec((1, HEAD_PAD)),
        ],
        out_specs=[row(_QKV_COLS, 0), row(_QKV_COLS, 0)],
        out_shape=[jax.ShapeDtypeStruct((t, _QKV_COLS), BF16)] * 2,
        compiler_params=_cparams("parallel"),
        name="mla_kv_up",
    )(cqkv, cqkv, cqkv, lora_norm.reshape(1, MLA_KV_RANK), wk, wv, gk, gra, grb, cos_t, sin_t, pm, ones)


_ATTN_UNROLL = 4


def _attn_kernel(q_ref, k_ref, v_ref, o_ref, s_ref, p_ref, mx_ref, m_ref, alpha_ref, acc_ref, *, tq, tk):
    qi = pl.program_id(2)

    def rows(j):
        return pl.ds(pl.multiple_of(j * tk, tk), tk)

    def stage_a(slot, j, limit=None, r0=0):
        nr = tq - r0
        s = _dot_nt(q_ref[r0:, :], k_ref[rows(j), :])
        if limit is not None:
            rel = (lax.broadcasted_iota(jnp.int32, (nr, tk), 1) - lax.broadcasted_iota(jnp.int32, (nr, tk), 0))
            s = jnp.where(rel <= limit + r0, s, -jnp.inf)
        s_ref[slot, r0:, :] = s
        mx_ref[slot, r0:, :] = jnp.broadcast_to(jnp.max(s, axis=-1, keepdims=True), (nr, LANES))

    def stage_b(slot, r0=0):
        m = m_ref[r0:, :]
        m_new = jnp.maximum(m, mx_ref[slot, r0:, :])
        for c in range(0, tk, LANES):
            pb = jnp.exp2(s_ref[slot, r0:, c:c + LANES] - m_new).astype(BF16)
            p_ref[slot, r0 // 2:, c:c + LANES] = pltpu.bitcast(pb, jnp.uint32)
        m_ref[r0:, :] = m_new
        alpha_ref[r0:, :] = jnp.exp2(m - m_new)

    def stage_c(slot, j, r0=0):
        p = pltpu.bitcast(p_ref[slot, r0 // 2:, :], BF16)
        acc_ref[r0:, :] = alpha_ref[r0:, :] * acc_ref[r0:, :] + _dot(p, v_ref[rows(j), :])

    def trip(ii, limit):
        j = 2 * ii
        stage_a(1, j + 1)
        stage_c(1, jnp.maximum(j - 1, 0))
        stage_b(0)
        stage_a(0, j + 2, limit=limit)
        stage_c(0, j)
        stage_b(1)

    p_ref[1] = jnp.zeros((tq // 2, tk), jnp.uint32)
    m_ref[...] = jnp.full((tq, LANES), -jnp.inf, F32)
    alpha_ref[...] = jnp.ones((tq, LANES), F32)
    acc_ref[...] = jnp.zeros((tq, HEAD_PAD), F32)
    stage_a(0, 0, limit=qi * tq)

    n_plain = jnp.maximum(qi - 1, 0)
    n_body = jnp.right_shift(n_plain, _ATTN_UNROLL.bit_length() - 1)

    @pl.loop(0, n_body)
    def _(ib):
        for u in range(_ATTN_UNROLL):
            trip(_ATTN_UNROLL * ib + u, None)

    for r in range(_ATTN_UNROLL - 1):
        @pl.when(n_plain - _ATTN_UNROLL * n_body > r)
        def _():
            trip(_ATTN_UNROLL * n_body + r, None)

    @pl.when(qi > 0)
    def _():
        trip(qi - 1, 0)

    jd = 2 * qi
    stage_a(1, jd + 1, limit=-tk, r0=tk)
    stage_c(1, jnp.maximum(jd - 1, 0))
    stage_b(0)
    stage_c(0, jd)
    stage_b(1, r0=tk)
    stage_c(1, jd + 1, r0=tk)
    acc = acc_ref[...]
    o_ref[...] = (acc / acc[:, MLA_V:MLA_V + 1]).astype(o_ref.dtype)


def flash_attention(q, k, v, *, batch, tk=512):
    t = q.shape[0]
    seqlen = t // batch
    tq = 2 * tk
    nq = seqlen // tq
    return pl.pallas_call(
        functools.partial(_attn_kernel, tq=tq, tk=tk),
        grid=(batch, MLA_HEADS, nq),
        in_specs=[
            pl.BlockSpec((tq, HEAD_PAD), lambda b, h, i: (b * nq + i, h)),
            pl.BlockSpec((seqlen, HEAD_PAD), lambda b, h, i: (b, h)),
            pl.BlockSpec((seqlen, HEAD_PAD), lambda b, h, i: (b, h)),
        ],
        out_specs=pl.BlockSpec((tq, HEAD_PAD), lambda b, h, i: (b * nq + i, h)),
        out_shape=jax.ShapeDtypeStruct((t, _QKV_COLS), BF16),
        scratch_shapes=[
            pltpu.VMEM((2, tq, tk), F32),
            pltpu.VMEM((2, tq // 2, tk), jnp.uint32),
            pltpu.VMEM((2, tq, LANES), F32),
            pltpu.VMEM((tq, LANES), F32),
            pltpu.VMEM((tq, LANES), F32),
            pltpu.VMEM((tq, HEAD_PAD), F32),
        ],
        compiler_params=_cparams("parallel", "parallel", "arbitrary"),
        name="mla_attention",
    )(q, k, v)


def _even_mixer(x, batch, norm_w, w_in, conv_w, conv_b, dt_bias, a_log, d_skip, ssd_norm,
                gk_w2, gk_b, gla_norm, w_out):
    sizes = (D_SSM, CONV_DIM, SSD_HEADS, GLA_KEY_DIM, GLA_KEY_DIM, GLA_VAL_DIM, GLA_VAL_DIM, GLA_GATE_RANK)
    offs = [0]
    for s in sizes:
        offs.append(offs[-1] + s)
    seg = [w_in[:, offs[i]:offs[i + 1]] for i in range(len(sizes))]
    w_z, w_xbc, w_dt, w_q, w_k, w_v, w_g, w_gkl = seg
    w_main = jnp.concatenate(
        [w_z, w_v, w_g, w_xbc, w_q, w_k, jnp.pad(w_gkl, ((0, 0), (0, LANES - GLA_GATE_RANK)))], axis=1).astype(BF16)
    w_dtp = jnp.pad(w_dt, ((0, 0), (0, LANES - SSD_HEADS))).astype(BF16)
    proj, dt_raw = norm_proj(x, norm_w, [w_main, w_dtp], [BF16, F32])
    y_ssd = ssd_scan(proj, dt_raw, conv_w, conv_b, dt_bias, a_log, d_skip, ssd_norm, batch=batch)
    y_gla = gla_scan(proj, gk_w2, gk_b, gla_norm, batch=batch)
    w_o = w_out.astype(BF16)
    return [y_ssd, y_gla], [w_o[:D_SSM], w_o[D_SSM:]]


def _odd_mixer(x, batch, cos_t, sin_t, norm_w, w_dqkv, q_lora_norm, w_uq, kv_lora_norm, w_ukv,
               q_nope_norm, q_rope_norm, k_nope_norm, k_rope_norm, w_o):
    w_cq = w_dqkv[:, :MLA_Q_RANK]
    w_ckv = w_dqkv[:, MLA_Q_RANK:MLA_Q_RANK + MLA_KV_RANK]
    w_kr = w_dqkv[:, MLA_Q_RANK + MLA_KV_RANK:]
    w_d = jnp.concatenate(
        [w_cq, jnp.zeros((D_MODEL, _DQ_CKV - MLA_Q_RANK), F32), w_ckv, _rope_lanes(w_kr),
         _rope_lanes(_swap_halves(w_kr))], axis=1).astype(BF16)
    (cqkv,) = norm_proj(x, norm_w, [w_d], [F32])
    q = q_up(cqkv, q_lora_norm, w_uq, q_nope_norm, q_rope_norm, cos_t, sin_t)
    k, v = kv_up(cqkv, kv_lora_norm, w_ukv, k_nope_norm, k_rope_norm, cos_t, sin_t)
    o = flash_attention(q, k, v, batch=batch)
    w_oe = jnp.pad(w_o.reshape(MLA_HEADS, MLA_V, D_MODEL), ((0, 0), (0, HEAD_PAD - MLA_V), (0, 0)))
    return [o], [w_oe.reshape(_QKV_COLS, D_MODEL).astype(BF16)]


def kernel(x, mix_norm_even, w_in_even, conv_w, conv_b, dt_bias, a_log, d_skip, ssd_norm, gla_gk_w2, gla_gk_b, gla_norm, w_out_even, mix_norm_odd, w_dqkv, q_lora_norm, w_uq, kv_lora_norm, w_ukv, q_nope_norm, q_rope_norm, k_nope_norm, k_rope_norm, w_o_mla, ffn_norm, w_gate, w_up, w_down):
    batch, seqlen, d = x.shape
    depth = ffn_norm.shape[0]
    cos_t, sin_t = _rope_tables(seqlen, batch)
    x = x.reshape(batch * seqlen, d)
    for i in range(depth):
        j = i // 2
        if i % 2 == 0:
            acts, w_outs = _even_mixer(x, batch, mix_norm_even[j], w_in_even[j], conv_w[j], conv_b[j], dt_bias[j],
                                       a_log[j], d_skip[j], ssd_norm[j], gla_gk_w2[j], gla_gk_b[j], gla_norm[j],
                                       w_out_even[j])
        else:
            acts, w_outs = _odd_mixer(x, batch, cos_t, sin_t, mix_norm_odd[j], w_dqkv[j], q_lora_norm[j], w_uq[j],
                                      kv_lora_norm[j], w_ukv[j], q_nope_norm[j], q_rope_norm[j], k_nope_norm[j],
                                      k_rope_norm[j], w_o_mla[j])
        x = mix_ffn_residual(x, acts, w_outs, ffn_norm[i], w_gate[i].astype(BF16), w_up[i].astype(BF16),
                             w_down[i].astype(BF16))
    return x.reshape(batch, seqlen, d)
```

```python
import functools
import math

import jax
import jax.numpy as jnp
from jax import lax
from jax.experimental import pallas as pl
from jax.experimental.pallas import tpu as pltpu

F32 = jnp.float32
BF16 = jnp.bfloat16

D_MODEL = 1024
NORM_EPS = 1e-6

SSD_HEAD_DIM = 64
SSD_HEADS = 16
SSD_GROUPS = 2
SSD_HPG = SSD_HEADS // SSD_GROUPS
SSD_STATE = 128
SSD_CHUNK = 128
CONV_WIDTH = 4
D_SSM = SSD_HEADS * SSD_HEAD_DIM
CONV_DIM = D_SSM + 2 * SSD_GROUPS * SSD_STATE

GLA_HEADS = 4
GLA_KEY_DIM = D_MODEL // 2
GLA_VAL_DIM = D_MODEL
GLA_HEAD_K = GLA_KEY_DIM // GLA_HEADS
GLA_HEAD_V = GLA_VAL_DIM // GLA_HEADS
GLA_GATE_RANK = 16
GLA_GATE_NORM = 16.0
GLA_CHUNK = 64

MLA_HEADS = 16
MLA_NOPE = 64
MLA_ROPE = 32
MLA_V = 64
MLA_Q_RANK = 384
MLA_KV_RANK = 256
ROPE_THETA = 10000.0

LANES = 128
HEAD_PAD = 128
V7X_VMEM_BYTES = 64 * 1024 * 1024
VMEM_LIMIT = V7X_VMEM_BYTES - 8 * 1024 * 1024


def _cparams(*sem):
    return pltpu.CompilerParams(dimension_semantics=sem, vmem_limit_bytes=VMEM_LIMIT)


def _dot(a, b):
    return jnp.dot(a, b, preferred_element_type=F32)


def _dot_nt(a, b):
    return lax.dot_general(a, b, (((1,), (1,)), ((), ())), preferred_element_type=F32)


def _dot_tn(a, b):
    return lax.dot_general(a, b, (((0,), (0,)), ((), ())), preferred_element_type=F32)


def _split3(x):
    hi = x.astype(BF16)
    r1 = x - hi.astype(F32)
    mid = r1.astype(BF16)
    lo = (r1 - mid.astype(F32)).astype(BF16)
    return hi, mid, lo


def _dot_sel(x, m):
    hi, mid, lo = _split3(x)
    return _dot(hi, m) + _dot(mid, m) + _dot(lo, m)


def _sel_dot(m, x):
    hi, mid, lo = _split3(x)
    return _dot(m, hi) + _dot(m, mid) + _dot(m, lo)


def _rms(x, w):
    return x * lax.rsqrt(jnp.mean(x * x, axis=-1, keepdims=True) + NORM_EPS) * w


def _silu(x):
    return x * jax.nn.sigmoid(x)


def _softplus(x):
    return jnp.maximum(x, 0.0) + jnp.log1p(jnp.exp(-jnp.abs(x)))


def _col_chunks(n, step):
    return [(c, min(c + step, n)) for c in range(0, n, step)]


def _const_spec(shape):
    return pl.BlockSpec(shape, lambda *_: (0,) * len(shape), pipeline_mode=pl.Buffered(1))


def _norm_proj_kernel(*refs, n_out, chunk):
    x_ref, nw_ref = refs[0], refs[1]
    w_refs = refs[2:2 + n_out]
    o_refs = refs[2 + n_out:]
    h = _rms(x_ref[...].astype(F32), nw_ref[...]).astype(BF16)
    for w_ref, o_ref in zip(w_refs, o_refs):
        for c0, c1 in _col_chunks(w_ref.shape[1], chunk):
            o_ref[:, c0:c1] = _dot(h, w_ref[:, c0:c1]).astype(o_ref.dtype)


def norm_proj(x, norm_w, weights, out_dtypes, *, tm=512, chunk=512):
    t, k = x.shape
    n_out = len(weights)
    in_specs = [pl.BlockSpec((tm, k), lambda i: (i, 0)), _const_spec((1, k))]
    in_specs += [_const_spec(w.shape) for w in weights]
    out_specs = [pl.BlockSpec((tm, w.shape[1]), lambda i: (i, 0)) for w in weights]
    out_shape = [jax.ShapeDtypeStruct((t, w.shape[1]), dt) for w, dt in zip(weights, out_dtypes)]
    return pl.pallas_call(
        functools.partial(_norm_proj_kernel, n_out=n_out, chunk=chunk),
        grid=(t // tm,),
        in_specs=in_specs,
        out_specs=out_specs,
        out_shape=out_shape,
        compiler_params=_cparams("parallel"),
        name="norm_proj",
    )(x, norm_w.reshape(1, k), *weights)


def _drop_group_padding(a, keep):
    return jnp.concatenate([a[:, g * LANES:g * LANES + keep] for g in range(a.shape[1] // LANES)], axis=1)


def _mix_ffn_kernel(*refs, n_in, chunk):
    x_ref = refs[0]
    a_refs = refs[1:1 + n_in]
    w_refs = refs[1 + n_in:1 + 2 * n_in]
    nw_ref, wg_ref, wu_ref, wd_ref, o_ref = refs[1 + 2 * n_in:]
    x = x_ref[...]
    for a_ref, w_ref in zip(a_refs, w_refs):
        a = a_ref[...]
        if a.shape[1] != w_ref.shape[0]:
            a = _drop_group_padding(a, w_ref.shape[0] * LANES // a.shape[1])
        x = x + _dot(a, w_ref[...])
    o_ref[...] = x
    x = o_ref[...]
    h = _rms(x, nw_ref[...]).astype(BF16)
    acc = x
    for c0, c1 in _col_chunks(wg_ref.shape[1], chunk):
        g = _dot(h, wg_ref[:, c0:c1])
        u = _dot(h, wu_ref[:, c0:c1])
        a = (_silu(g) * u).astype(BF16)
        acc = acc + _dot(a, wd_ref[c0:c1, :])
    o_ref[...] = acc


def mix_ffn_residual(x, acts, weights, norm_w, wg, wu, wd, *, tm=512, chunk=256):
    t, d = x.shape
    n_in = len(acts)
    row = pl.BlockSpec((tm, d), lambda i: (i, 0))
    in_specs = [row]
    in_specs += [pl.BlockSpec((tm, a.shape[1]), lambda i: (i, 0)) for a in acts]
    in_specs += [_const_spec(w.shape) for w in weights]
    in_specs += [_const_spec((1, d)), _const_spec(wg.shape), _const_spec(wu.shape), _const_spec(wd.shape)]
    return pl.pallas_call(
        functools.partial(_mix_ffn_kernel, n_in=n_in, chunk=chunk),
        grid=(t // tm,),
        in_specs=in_specs,
        out_specs=row,
        out_shape=jax.ShapeDtypeStruct((t, d), F32),
        compiler_params=_cparams("parallel"),
        name="mix_ffn_residual",
    )(x, *acts, *weights, norm_w.reshape(1, d), wg, wu, wd)


_COL_Z = 0
_COL_V = _COL_Z + D_SSM
_COL_G = _COL_V + GLA_VAL_DIM
_COL_XBC = _COL_G + GLA_VAL_DIM
_COL_Q = _COL_XBC + CONV_DIM
_COL_K = _COL_Q + GLA_KEY_DIM
_COL_GKL = _COL_K + GLA_KEY_DIM
_PROJ_COLS = _COL_GKL + LANES
_CONV_HALO = 8


def _ssd_kernel(z_ref, xs_ref, bc_ref, dt_ref, cw_ref, cb_ref, dtb_ref, nega_ref, dskip_ref, nw_ref,
                tril_ref, exp_ref, rep_ref, o_ref, xpad_ref, state_ref):
    q = SSD_CHUNK
    p = SSD_HEAD_DIM
    n = SSD_STATE
    gw = SSD_HPG * p

    @pl.when(pl.program_id(1) == 0)
    def _():
        xpad_ref[0:_CONV_HALO, :] = jnp.zeros((_CONV_HALO, CONV_DIM), F32)
        state_ref[...] = jnp.zeros_like(state_ref)

    xpad_ref[_CONV_HALO:, 0:D_SSM] = xs_ref[...].astype(F32)
    xpad_ref[_CONV_HALO:, D_SSM:] = bc_ref[...].astype(F32)
    conv = cb_ref[...]
    for j in range(CONV_WIDTH):
        off = _CONV_HALO - (CONV_WIDTH - 1) + j
        conv = conv + cw_ref[j:j + 1, :] * xpad_ref[off:off + q, :]
    xpad_ref[0:_CONV_HALO, :] = xpad_ref[q:q + _CONV_HALO, :]
    xbc = _silu(conv)

    dt = _softplus(dt_ref[...] + dtb_ref[...])
    a = dt * nega_ref[...]
    acum = _sel_dot(tril_ref[...], a)
    acum_t = acum.T
    dt_x = _dot_sel(dt, exp_ref[...])
    acum_x = _dot_sel(acum, exp_ref[...])
    acum_rep = _dot_sel(acum, rep_ref[...])
    a_last_x = acum_x[q - 1:q, :]
    xs = xbc[:, 0:D_SSM]
    xdt = xs * dt_x
    xdt_bf = xdt.astype(BF16)
    w_end = (xdt * jnp.exp(a_last_x - acum_x)).astype(BF16)
    exp_a = jnp.exp(acum_x)
    chunk_decay = jnp.exp(a_last_x)
    y_skip = xs * dskip_ref[...]
    row = lax.broadcasted_iota(jnp.int32, (q, q), 0)
    col = lax.broadcasted_iota(jnp.int32, (q, q), 1)
    causal = row >= col
    lane = lax.broadcasted_iota(jnp.int32, (q, 2 * p), 1)

    y_groups = []
    for g in range(SSD_GROUPS):
        gs = slice(g * gw, (g + 1) * gw)
        b_g = xbc[:, D_SSM + g * n:D_SSM + (g + 1) * n]
        c_g = xbc[:, D_SSM + (SSD_GROUPS + g) * n:D_SSM + (SSD_GROUPS + g + 1) * n]
        b_bf = b_g.astype(BF16)
        c_bf = c_g.astype(BF16)
        cb = _dot_nt(c_bf, b_bf)
        s_g = state_ref[g]
        y_off = _dot(c_bf, s_g.astype(BF16)) * exp_a[:, gs]
        y_pairs = []
        for k in range(0, SSD_HPG, 2):
            lhs = []
            for hk in (g * SSD_HPG + k, g * SSD_HPG + k + 1):
                a_col = acum_rep[:, hk * q:(hk + 1) * q]
                a_row = acum_t[hk:hk + 1, :]
                decay = jnp.where(causal, jnp.exp(jnp.minimum(a_col - a_row, 0.0)), 0.0)
                lhs.append((cb * decay).astype(BF16))
            hk0 = g * SSD_HPG + k
            x_pair = xdt_bf[:, hk0 * p:(hk0 + 2) * p]
            zero = jnp.zeros_like(x_pair)
            rhs = jnp.concatenate([jnp.where(lane < p, x_pair, zero), jnp.where(lane >= p, x_pair, zero)], axis=0)
            y_pairs.append(_dot(jnp.concatenate(lhs, axis=1), rhs))
        state_ref[g] = s_g * chunk_decay[:, gs] + _dot_tn(b_bf, w_end[:, gs])
        y_g = jnp.concatenate(y_pairs, axis=1) + y_off + y_skip[:, gs]
        y_g = y_g * _silu(z_ref[:, gs].astype(F32))
        y_groups.append(_rms(y_g, nw_ref[:, gs]))
    o_ref[...] = jnp.concatenate(y_groups, axis=1).astype(o_ref.dtype)


def ssd_scan(proj, dt_raw, conv_w, conv_b, dt_bias, a_log, d_skip, norm_w, *, batch):
    t = proj.shape[0]
    q = SSD_CHUNK
    nc = t // batch // q

    def pad_heads(v):
        return jnp.pad(v.reshape(1, SSD_HEADS), ((0, 0), (0, LANES - SSD_HEADS)))

    tril = jnp.tril(jnp.ones((q, q), F32)).astype(BF16)
    head = jnp.arange(LANES)[:, None]
    expand = (head == jnp.arange(D_SSM)[None, :] // SSD_HEAD_DIM).astype(BF16)
    replicate = (head == jnp.arange(SSD_HEADS * q)[None, :] // q).astype(BF16)
    rows = lambda b, c: b * nc + c
    return pl.pallas_call(
        _ssd_kernel,
        grid=(batch, nc),
        in_specs=[
            pl.BlockSpec((q, D_SSM), lambda b, c: (rows(b, c), _COL_Z // D_SSM)),
            pl.BlockSpec((q, D_SSM), lambda b, c: (rows(b, c), _COL_XBC // D_SSM)),
            pl.BlockSpec((q, CONV_DIM - D_SSM), lambda b, c: (rows(b, c), (_COL_XBC + D_SSM) // (CONV_DIM - D_SSM))),
            pl.BlockSpec((q, LANES), lambda b, c: (rows(b, c), 0)),
            _const_spec((CONV_WIDTH, CONV_DIM)),
            _const_spec((1, CONV_DIM)),
            _const_spec((1, LANES)),
            _const_spec((1, LANES)),
            _const_spec((1, D_SSM)),
            _const_spec((1, D_SSM)),
            _const_spec((q, q)),
            _const_spec(expand.shape),
            _const_spec(replicate.shape),
        ],
        out_specs=pl.BlockSpec((q, D_SSM), lambda b, c: (rows(b, c), 0)),
        out_shape=jax.ShapeDtypeStruct((t, D_SSM), BF16),
        scratch_shapes=[
            pltpu.VMEM((q + _CONV_HALO, CONV_DIM), F32),
            pltpu.VMEM((SSD_GROUPS, SSD_STATE, SSD_HPG * SSD_HEAD_DIM), F32),
        ],
        compiler_params=_cparams("parallel", "arbitrary"),
        name="ssd_scan",
    )(proj, proj, proj, dt_raw, conv_w, conv_b.reshape(1, CONV_DIM), pad_heads(dt_bias),
      pad_heads(-jnp.exp(a_log)), jnp.repeat(d_skip, SSD_HEAD_DIM).reshape(1, D_SSM), norm_w.reshape(1, D_SSM),
      tril, expand, replicate)


def _gla_kernel(q_ref, k_ref, v_ref, g_ref, gkl_ref, w2_ref, gkb_ref, nw_ref, tril_ref, o_ref, state_ref,
                *, block):
    c = GLA_CHUNK
    dk, dv = GLA_HEAD_K, GLA_HEAD_V
    scale = dk ** -0.5

    @pl.when(pl.program_id(1) == 0)
    def _():
        state_ref[...] = jnp.zeros_like(state_ref)

    gk = -_softplus(-(_dot(gkl_ref[...], w2_ref[...]) + gkb_ref[...])) * (1.0 / GLA_GATE_NORM)
    row = lax.broadcasted_iota(jnp.int32, (c, c), 0)
    col = lax.broadcasted_iota(jnp.int32, (c, c), 1)
    causal = row >= col
    for j in range(block // c):
        r0, r1 = j * c, (j + 1) * c
        gcum_all = _sel_dot(tril_ref[...], gk[r0:r1])
        for h in range(GLA_HEADS):
            ks, vs = slice(h * dk, (h + 1) * dk), slice(h * dv, (h + 1) * dv)
            gcum = gcum_all[:, ks]
            g_last = gcum[c - 1:c, :]
            qc = q_ref[r0:r1, ks].astype(F32) * scale
            kc = k_ref[r0:r1, ks].astype(F32)
            vc = v_ref[r0:r1, vs]
            q_dec = (qc * jnp.exp(gcum)).astype(BF16)
            k_inv = (kc * jnp.exp(-gcum)).astype(BF16)
            k_end = (kc * jnp.exp(g_last - gcum)).astype(BF16)
            scores = jnp.where(causal, _dot_nt(q_dec, k_inv), 0.0)
            s_t = state_ref[h]
            o = _dot(scores.astype(BF16), vc) + _dot_nt(q_dec, s_t.astype(BF16))
            state_ref[h] = s_t * jnp.exp(g_last) + _dot_tn(vc, k_end)
            o = _rms(o, nw_ref[...]) * _silu(g_ref[r0:r1, vs].astype(F32))
            o_ref[r0:r1, vs] = o.astype(o_ref.dtype)


def gla_scan(proj, gk_w2, gk_b, norm_w, *, batch, block=256):
    t = proj.shape[0]
    nb = t // batch // block
    kd, vd = GLA_KEY_DIM, GLA_VAL_DIM
    w2 = jnp.pad(gk_w2, ((0, LANES - GLA_GATE_RANK), (0, 0))).astype(BF16)
    tril = jnp.tril(jnp.ones((GLA_CHUNK, GLA_CHUNK), F32)).astype(BF16)
    rows = lambda b, i: b * nb + i
    return pl.pallas_call(
        functools.partial(_gla_kernel, block=block),
        grid=(batch, nb),
        in_specs=[
            pl.BlockSpec((block, kd), lambda b, i: (rows(b, i), _COL_Q // kd)),
            pl.BlockSpec((block, kd), lambda b, i: (rows(b, i), _COL_K // kd)),
            pl.BlockSpec((block, vd), lambda b, i: (rows(b, i), _COL_V // vd)),
            pl.BlockSpec((block, vd), lambda b, i: (rows(b, i), _COL_G // vd)),
            pl.BlockSpec((block, LANES), lambda b, i: (rows(b, i), _COL_GKL // LANES)),
            _const_spec((LANES, kd)),
            _const_spec((1, kd)),
            _const_spec((1, GLA_HEAD_V)),
            _const_spec((GLA_CHUNK, GLA_CHUNK)),
        ],
        out_specs=pl.BlockSpec((block, vd), lambda b, i: (rows(b, i), 0)),
        out_shape=jax.ShapeDtypeStruct((t, vd), BF16),
        scratch_shapes=[pltpu.VMEM((GLA_HEADS, GLA_HEAD_V, GLA_HEAD_K), F32)],
        compiler_params=_cparams("parallel", "arbitrary"),
        name="gla_scan",
    )(proj, proj, proj, proj, proj, w2, gk_b.reshape(1, kd), norm_w.reshape(1, GLA_HEAD_V), tril)


_DQ_CQ = 0
_DQ_CKV = 512
_DQ_KRA = 768
_DQ_KRB = 896
_DQ_COLS = 1024
_QKV_COLS = MLA_HEADS * HEAD_PAD
V7X_MXU_COLS = 256
_HEAD_GROUP = V7X_MXU_COLS // HEAD_PAD
_GROUP_COLS = _HEAD_GROUP * HEAD_PAD


def _mla_qkv_kernel(x_ref, nw_ref, wd_ref, qlw_ref, wa_ref, wb_ref, ga_ref, gb_ref, klw_ref, wk_ref, wv_ref,
                    gk_ref, gra_ref, grb_ref, cos_ref, sin_ref, pm_ref, ones_ref, q_ref, k_ref, v_ref, c_ref,
                    *, scale):
    h = _rms(x_ref[...], nw_ref[...]).astype(BF16)
    for c0, c1 in _col_chunks(_DQ_COLS, 512):
        c_ref[:, c0:c1] = _dot(h, wd_ref[:, c0:c1])
    cos = cos_ref[...]
    sin = sin_ref[...]
    pm = pm_ref[...]

    cq = _rms(c_ref[:, _DQ_CQ:_DQ_CQ + MLA_Q_RANK], qlw_ref[...]).astype(BF16)
    fa = jnp.concatenate([ga_ref[...] * cos * scale] * _HEAD_GROUP, axis=1)
    fb = jnp.concatenate([gb_ref[...] * sin * scale] * _HEAD_GROUP, axis=1)
    for g in range(MLA_HEADS // _HEAD_GROUP):
        sl = slice(g * _GROUP_COLS, (g + 1) * _GROUP_COLS)
        a = _dot(cq, wa_ref[:, sl])
        b = _dot(cq, wb_ref[:, sl])
        ms = _dot((a * a).astype(BF16), pm)
        q_ref[:, sl] = ((a * fa + b * fb) * lax.rsqrt(ms + NORM_EPS)).astype(q_ref.dtype)

    ckv = _rms(c_ref[:, _DQ_CKV:_DQ_CKV + MLA_KV_RANK], klw_ref[...]).astype(BF16)
    kra = c_ref[:, _DQ_KRA:_DQ_KRA + HEAD_PAD]
    krb = c_ref[:, _DQ_KRB:_DQ_KRB + HEAD_PAD]
    r_rope = lax.rsqrt(jnp.sum(kra * kra, axis=-1, keepdims=True) * (1.0 / MLA_ROPE) + NORM_EPS)
    k_rope = (kra * (gra_ref[...] * cos) + krb * (grb_ref[...] * sin)) * r_rope
    k_rope = jnp.concatenate([k_rope] * _HEAD_GROUP, axis=1)
    gk = jnp.concatenate([gk_ref[...]] * _HEAD_GROUP, axis=1)
    ones = jnp.concatenate([ones_ref[...]] * _HEAD_GROUP, axis=1)
    for g in range(MLA_HEADS // _HEAD_GROUP):
        sl = slice(g * _GROUP_COLS, (g + 1) * _GROUP_COLS)
        kn = _dot(ckv, wk_ref[:, sl])
        ms = _dot((kn * kn).astype(BF16), pm)
        k_ref[:, sl] = (kn * gk * lax.rsqrt(ms + NORM_EPS) + k_rope).astype(k_ref.dtype)
        v_ref[:, sl] = (_dot(ckv, wv_ref[:, sl]) + ones).astype(v_ref.dtype)


def _part_mean_matrix():
    lane = jnp.arange(_GROUP_COLS)
    head, pos = lane // HEAD_PAD, lane % HEAD_PAD
    part = jnp.where(pos < MLA_NOPE, 0, jnp.where(pos < MLA_NOPE + MLA_ROPE, 1, 2))
    size = jnp.where(part == 0, MLA_NOPE, MLA_ROPE).astype(F32)
    same = (head[:, None] == head[None, :]) & (part[:, None] == part[None, :]) & (part[:, None] < 2)
    return (same.astype(F32) / size[None, :]).astype(BF16)


def _rope_tables(seqlen, batch):
    pos = jnp.arange(seqlen, dtype=F32)
    inv = 1.0 / (ROPE_THETA ** (jnp.arange(0, MLA_ROPE, 2, dtype=F32) / MLA_ROPE))
    ang = pos[:, None] * inv[None, :]
    cos, sin = jnp.cos(ang), jnp.sin(ang)
    pad = jnp.zeros((seqlen, HEAD_PAD - MLA_NOPE - MLA_ROPE), F32)
    cos_t = jnp.concatenate([jnp.ones((seqlen, MLA_NOPE), F32), cos, cos, pad], axis=1)
    sin_t = jnp.concatenate([jnp.zeros((seqlen, MLA_NOPE), F32), -sin, sin, pad], axis=1)
    return jnp.tile(cos_t, (batch, 1)), jnp.tile(sin_t, (batch, 1))


def _swap_halves(w):
    half = w.shape[-1] // 2
    return jnp.concatenate([w[..., half:], w[..., :half]], axis=-1)


def _rope_lanes(w):
    pad = [(0, 0)] * (w.ndim - 1) + [(MLA_NOPE, HEAD_PAD - MLA_NOPE - MLA_ROPE)]
    return jnp.pad(w, pad)


def mla_qkv(x, norm_w, w_dqkv, q_lora_norm, w_uq, kv_lora_norm, w_ukv, q_nope_norm, q_rope_norm,
            k_nope_norm, k_rope_norm, cos_t, sin_t, *, tm=512):
    t, d = x.shape
    pad_rope = HEAD_PAD - MLA_NOPE - MLA_ROPE
    w_cq = w_dqkv[:, :MLA_Q_RANK]
    w_ckv = w_dqkv[:, MLA_Q_RANK:MLA_Q_RANK + MLA_KV_RANK]
    w_kr = w_dqkv[:, MLA_Q_RANK + MLA_KV_RANK:]
    wd = jnp.concatenate(
        [w_cq, jnp.zeros((d, _DQ_CKV - MLA_Q_RANK), F32), w_ckv, _rope_lanes(w_kr),
         _rope_lanes(_swap_halves(w_kr))], axis=1).astype(BF16)
    wq = w_uq.reshape(MLA_Q_RANK, MLA_HEADS, MLA_NOPE + MLA_ROPE)
    wq_nope, wq_rope = wq[..., :MLA_NOPE], wq[..., MLA_NOPE:]
    wa = jnp.concatenate([wq_nope, wq_rope, jnp.zeros((MLA_Q_RANK, MLA_HEADS, pad_rope), F32)], axis=-1)
    wa = wa.reshape(MLA_Q_RANK, _QKV_COLS).astype(BF16)
    wb = _rope_lanes(_swap_halves(wq_rope)).reshape(MLA_Q_RANK, _QKV_COLS).astype(BF16)
    ga = jnp.concatenate([q_nope_norm, q_rope_norm, jnp.zeros((pad_rope,), F32)]).reshape(1, HEAD_PAD)
    gb = _rope_lanes(_swap_halves(q_rope_norm)).reshape(1, HEAD_PAD)
    wkv = w_ukv.reshape(MLA_KV_RANK, MLA_HEADS, MLA_NOPE + MLA_V)
    wk = jnp.pad(wkv[..., :MLA_NOPE], [(0, 0), (0, 0), (0, HEAD_PAD - MLA_NOPE)])
    wv = jnp.pad(wkv[..., MLA_NOPE:], [(0, 0), (0, 0), (0, HEAD_PAD - MLA_V)])
    wk = wk.reshape(MLA_KV_RANK, _QKV_COLS).astype(BF16)
    wv = wv.reshape(MLA_KV_RANK, _QKV_COLS).astype(BF16)
    gk = jnp.pad(k_nope_norm, (0, HEAD_PAD - MLA_NOPE)).reshape(1, HEAD_PAD)
    gra = _rope_lanes(k_rope_norm).reshape(1, HEAD_PAD)
    grb = _rope_lanes(_swap_halves(k_rope_norm)).reshape(1, HEAD_PAD)
    ones = (jnp.arange(HEAD_PAD) >= MLA_V).astype(F32).reshape(1, HEAD_PAD)
    pm = _part_mean_matrix()
    row = lambda width: pl.BlockSpec((tm, width), lambda i: (i, 0))
    vec = lambda n: _const_spec((1, n))
    return pl.pallas_call(
        functools.partial(_mla_qkv_kernel, scale=(MLA_NOPE + MLA_ROPE) ** -0.5 * math.log2(math.e)),
        grid=(t // tm,),
        in_specs=[
            row(d), vec(d), _const_spec(wd.shape),
            vec(MLA_Q_RANK), _const_spec(wa.shape), _const_spec(wb.shape), vec(HEAD_PAD), vec(HEAD_PAD),
            vec(MLA_KV_RANK), _const_spec(wk.shape), _const_spec(wv.shape), vec(HEAD_PAD), vec(HEAD_PAD), vec(HEAD_PAD),
            row(HEAD_PAD), row(HEAD_PAD), _const_spec(pm.shape), vec(HEAD_PAD),
        ],
        out_specs=[row(_QKV_COLS)] * 3,
        out_shape=[jax.ShapeDtypeStruct((t, _QKV_COLS), BF16)] * 3,
        scratch_shapes=[pltpu.VMEM((tm, _DQ_COLS), F32)],
        compiler_params=_cparams("parallel"),
        name="mla_qkv",
    )(x, norm_w.reshape(1, d), wd, q_lora_norm.reshape(1, MLA_Q_RANK), wa, wb, ga, gb,
      kv_lora_norm.reshape(1, MLA_KV_RANK), wk, wv, gk, gra, grb, cos_t, sin_t, pm, ones)


_ATTN_UNROLL = 4


def _attn_kernel(q_ref, k_ref, v_ref, o_ref, s_ref, p_ref, mx_ref, m_ref, alpha_ref, acc_ref, *, tq, tk):
    qi = pl.program_id(2)

    def rows(j):
        return pl.ds(pl.multiple_of(j * tk, tk), tk)

    def stage_a(slot, j, limit=None, r0=0):
        nr = tq - r0
        s = _dot_nt(q_ref[r0:, :], k_ref[rows(j), :])
        if limit is not None:
            rel = (lax.broadcasted_iota(jnp.int32, (nr, tk), 1) - lax.broadcasted_iota(jnp.int32, (nr, tk), 0))
            s = jnp.where(rel <= limit + r0, s, -jnp.inf)
        s_ref[slot, r0:, :] = s
        mx_ref[slot, r0:, :] = jnp.broadcast_to(jnp.max(s, axis=-1, keepdims=True), (nr, LANES))

    def stage_b(slot, r0=0):
        m = m_ref[r0:, :]
        m_new = jnp.maximum(m, mx_ref[slot, r0:, :])
        for c in range(0, tk, LANES):
            pb = jnp.exp2(s_ref[slot, r0:, c:c + LANES] - m_new).astype(BF16)
            p_ref[slot, r0 // 2:, c:c + LANES] = pltpu.bitcast(pb, jnp.uint32)
        m_ref[r0:, :] = m_new
        alpha_ref[r0:, :] = jnp.exp2(m - m_new)

    def stage_c(slot, j, r0=0):
        p = pltpu.bitcast(p_ref[slot, r0 // 2:, :], BF16)
        acc_ref[r0:, :] = alpha_ref[r0:, :] * acc_ref[r0:, :] + _dot(p, v_ref[rows(j), :])

    def trip(ii, limit):
        j = 2 * ii
        stage_a(1, j + 1)
        stage_c(1, jnp.maximum(j - 1, 0))
        stage_b(0)
        stage_a(0, j + 2, limit=limit)
        stage_c(0, j)
        stage_b(1)

    p_ref[1] = jnp.zeros((tq // 2, tk), jnp.uint32)
    m_ref[...] = jnp.full((tq, LANES), -jnp.inf, F32)
    alpha_ref[...] = jnp.ones((tq, LANES), F32)
    acc_ref[...] = jnp.zeros((tq, HEAD_PAD), F32)
    stage_a(0, 0, limit=qi * tq)

    n_plain = jnp.maximum(qi - 1, 0)
    n_body = jnp.right_shift(n_plain, _ATTN_UNROLL.bit_length() - 1)

    @pl.loop(0, n_body)
    def _(ib):
        for u in range(_ATTN_UNROLL):
            trip(_ATTN_UNROLL * ib + u, None)

    for r in range(_ATTN_UNROLL - 1):
        @pl.when(n_plain - _ATTN_UNROLL * n_body > r)
        def _():
            trip(_ATTN_UNROLL * n_body + r, None)

    @pl.when(qi > 0)
    def _():
        trip(qi - 1, 0)

    jd = 2 * qi
    stage_a(1, jd + 1, limit=-tk, r0=tk)
    stage_c(1, jnp.maximum(jd - 1, 0))
    stage_b(0)
    stage_c(0, jd)
    stage_b(1, r0=tk)
    stage_c(1, jd + 1, r0=tk)
    acc = acc_ref[...]
    o_ref[...] = (acc / acc[:, MLA_V:MLA_V + 1]).astype(o_ref.dtype)


def flash_attention(q, k, v, *, batch, tk=512):
    t = q.shape[0]
    seqlen = t // batch
    tq = 2 * tk
    nq = seqlen // tq
    return pl.pallas_call(
        functools.partial(_attn_kernel, tq=tq, tk=tk),
        grid=(batch, MLA_HEADS, nq),
        in_specs=[
            pl.BlockSpec((tq, HEAD_PAD), lambda b, h, i: (b * nq + i, h)),
            pl.BlockSpec((seqlen, HEAD_PAD), lambda b, h, i: (b, h)),
            pl.BlockSpec((seqlen, HEAD_PAD), lambda b, h, i: (b, h)),
        ],
        out_specs=pl.BlockSpec((tq, HEAD_PAD), lambda b, h, i: (b * nq + i, h)),
        out_shape=jax.ShapeDtypeStruct((t, _QKV_COLS), BF16),
        scratch_shapes=[
            pltpu.VMEM((2, tq, tk), F32),
            pltpu.VMEM((2, tq // 2, tk), jnp.uint32),
            pltpu.VMEM((2, tq, LANES), F32),
            pltpu.VMEM((tq, LANES), F32),
            pltpu.VMEM((tq, LANES), F32),
            pltpu.VMEM((tq, HEAD_PAD), F32),
        ],
        compiler_params=_cparams("parallel", "parallel", "arbitrary"),
        name="mla_attention",
    )(q, k, v)


def _even_mixer(x, batch, norm_w, w_in, conv_w, conv_b, dt_bias, a_log, d_skip, ssd_norm,
                gk_w2, gk_b, gla_norm, w_out):
    sizes = (D_SSM, CONV_DIM, SSD_HEADS, GLA_KEY_DIM, GLA_KEY_DIM, GLA_VAL_DIM, GLA_VAL_DIM, GLA_GATE_RANK)
    offs = [0]
    for s in sizes:
        offs.append(offs[-1] + s)
    seg = [w_in[:, offs[i]:offs[i + 1]] for i in range(len(sizes))]
    w_z, w_xbc, w_dt, w_q, w_k, w_v, w_g, w_gkl = seg
    w_main = jnp.concatenate(
        [w_z, w_v, w_g, w_xbc, w_q, w_k, jnp.pad(w_gkl, ((0, 0), (0, LANES - GLA_GATE_RANK)))], axis=1).astype(BF16)
    w_dtp = jnp.pad(w_dt, ((0, 0), (0, LANES - SSD_HEADS))).astype(BF16)
    proj, dt_raw = norm_proj(x, norm_w, [w_main, w_dtp], [BF16, F32])
    y_ssd = ssd_scan(proj, dt_raw, conv_w, conv_b, dt_bias, a_log, d_skip, ssd_norm, batch=batch)
    y_gla = gla_scan(proj, gk_w2, gk_b, gla_norm, batch=batch)
    w_o = w_out.astype(BF16)
    return [y_ssd, y_gla], [w_o[:D_SSM], w_o[D_SSM:]]


def _odd_mixer(x, batch, cos_t, sin_t, norm_w, w_dqkv, q_lora_norm, w_uq, kv_lora_norm, w_ukv,
               q_nope_norm, q_rope_norm, k_nope_norm, k_rope_norm, w_o):
    q, k, v = mla_qkv(x, norm_w, w_dqkv, q_lora_norm, w_uq, kv_lora_norm, w_ukv, q_nope_norm, q_rope_norm,
                      k_nope_norm, k_rope_norm, cos_t, sin_t)
    o = flash_attention(q, k, v, batch=batch)
    return [o], [w_o.astype(BF16)]


def kernel(x, mix_norm_even, w_in_even, conv_w, conv_b, dt_bias, a_log, d_skip, ssd_norm, gla_gk_w2, gla_gk_b, gla_norm, w_out_even, mix_norm_odd, w_dqkv, q_lora_norm, w_uq, kv_lora_norm, w_ukv, q_nope_norm, q_rope_norm, k_nope_norm, k_rope_norm, w_o_mla, ffn_norm, w_gate, w_up, w_down):
    batch, seqlen, d = x.shape
    depth = ffn_norm.shape[0]
    cos_t, sin_t = _rope_tables(seqlen, batch)
    x = x.reshape(batch * seqlen, d)
    for i in range(depth):
        j = i // 2
        if i % 2 == 0:
            acts, w_outs = _even_mixer(x, batch, mix_norm_even[j], w_in_even[j], conv_w[j], conv_b[j], dt_bias[j],
                                       a_log[j], d_skip[j], ssd_norm[j], gla_gk_w2[j], gla_gk_b[j], gla_norm[j],
                                       w_out_even[j])
        else:
            acts, w_outs = _odd_mixer(x, batch, cos_t, sin_t, mix_norm_odd[j], w_dqkv[j], q_lora_norm[j], w_uq[j],
                                      kv_lora_norm[j], w_ukv[j], q_nope_norm[j], q_rope_norm[j], k_nope_norm[j],
                                      k_rope_norm[j], w_o_mla[j])
        x = mix_ffn_residual(x, acts, w_outs, ffn_norm[i], w_gate[i].astype(BF16), w_up[i].astype(BF16),
                             w_down[i].astype(BF16))
    return x.reshape(batch, seqlen, d)
```

```python
import functools
import math

import jax
import jax.numpy as jnp
from jax import lax
from jax.experimental import pallas as pl
from jax.experimental.pallas import tpu as pltpu

F32 = jnp.float32
BF16 = jnp.bfloat16

D_MODEL = 1024
NORM_EPS = 1e-6

SSD_HEAD_DIM = 64
SSD_HEADS = 16
SSD_GROUPS = 2
SSD_HPG = SSD_HEADS // SSD_GROUPS
SSD_STATE = 128
SSD_CHUNK = 128
CONV_WIDTH = 4
D_SSM = SSD_HEADS * SSD_HEAD_DIM
CONV_DIM = D_SSM + 2 * SSD_GROUPS * SSD_STATE

GLA_HEADS = 4
GLA_KEY_DIM = D_MODEL // 2
GLA_VAL_DIM = D_MODEL
GLA_HEAD_K = GLA_KEY_DIM // GLA_HEADS
GLA_HEAD_V = GLA_VAL_DIM // GLA_HEADS
GLA_GATE_RANK = 16
GLA_GATE_NORM = 16.0
GLA_CHUNK = 64

MLA_HEADS = 16
MLA_NOPE = 64
MLA_ROPE = 32
MLA_V = 64
MLA_Q_RANK = 384
MLA_KV_RANK = 256
ROPE_THETA = 10000.0

LANES = 128
HEAD_PAD = 128
V7X_VMEM_BYTES = 64 * 1024 * 1024
VMEM_LIMIT = V7X_VMEM_BYTES - 8 * 1024 * 1024


def _cparams(*sem):
    return pltpu.CompilerParams(dimension_semantics=sem, vmem_limit_bytes=VMEM_LIMIT)


def _dot(a, b):
    return jnp.dot(a, b, preferred_element_type=F32)


def _dot_nt(a, b):
    return lax.dot_general(a, b, (((1,), (1,)), ((), ())), preferred_element_type=F32)


def _dot_tn(a, b):
    return lax.dot_general(a, b, (((0,), (0,)), ((), ())), preferred_element_type=F32)


def _split3(x):
    hi = x.astype(BF16)
    r1 = x - hi.astype(F32)
    mid = r1.astype(BF16)
    lo = (r1 - mid.astype(F32)).astype(BF16)
    return hi, mid, lo


def _dot_sel(x, m):
    hi, mid, lo = _split3(x)
    return _dot(hi, m) + _dot(mid, m) + _dot(lo, m)


def _sel_dot(m, x):
    hi, mid, lo = _split3(x)
    return _dot(m, hi) + _dot(m, mid) + _dot(m, lo)


def _rms(x, w):
    return x * lax.rsqrt(jnp.mean(x * x, axis=-1, keepdims=True) + NORM_EPS) * w


def _silu(x):
    return x * jax.nn.sigmoid(x)


def _softplus(x):
    return jnp.maximum(x, 0.0) + jnp.log1p(jnp.exp(-jnp.abs(x)))


def _col_chunks(n, step):
    return [(c, min(c + step, n)) for c in range(0, n, step)]


def _const_spec(shape):
    return pl.BlockSpec(shape, lambda *_: (0,) * len(shape), pipeline_mode=pl.Buffered(1))


def _norm_proj_kernel(*refs, n_out, chunk):
    x_ref, nw_ref = refs[0], refs[1]
    w_refs = refs[2:2 + n_out]
    o_refs = refs[2 + n_out:]
    h = _rms(x_ref[...].astype(F32), nw_ref[...]).astype(BF16)
    for w_ref, o_ref in zip(w_refs, o_refs):
        for c0, c1 in _col_chunks(w_ref.shape[1], chunk):
            o_ref[:, c0:c1] = _dot(h, w_ref[:, c0:c1]).astype(o_ref.dtype)


def norm_proj(x, norm_w, weights, out_dtypes, *, tm=512, chunk=512):
    t, k = x.shape
    n_out = len(weights)
    in_specs = [pl.BlockSpec((tm, k), lambda i: (i, 0)), _const_spec((1, k))]
    in_specs += [_const_spec(w.shape) for w in weights]
    out_specs = [pl.BlockSpec((tm, w.shape[1]), lambda i: (i, 0)) for w in weights]
    out_shape = [jax.ShapeDtypeStruct((t, w.shape[1]), dt) for w, dt in zip(weights, out_dtypes)]
    return pl.pallas_call(
        functools.partial(_norm_proj_kernel, n_out=n_out, chunk=chunk),
        grid=(t // tm,),
        in_specs=in_specs,
        out_specs=out_specs,
        out_shape=out_shape,
        compiler_params=_cparams("parallel"),
        name="norm_proj",
    )(x, norm_w.reshape(1, k), *weights)


def _drop_group_padding(a, keep):
    return jnp.concatenate([a[:, g * LANES:g * LANES + keep] for g in range(a.shape[1] // LANES)], axis=1)


def _mix_ffn_kernel(*refs, n_in, chunk):
    x_ref = refs[0]
    a_refs = refs[1:1 + n_in]
    w_refs = refs[1 + n_in:1 + 2 * n_in]
    nw_ref, wg_ref, wu_ref, wd_ref, o_ref = refs[1 + 2 * n_in:]
    x = x_ref[...]
    for a_ref, w_ref in zip(a_refs, w_refs):
        a = a_ref[...]
        if a.shape[1] != w_ref.shape[0]:
            a = _drop_group_padding(a, w_ref.shape[0] * LANES // a.shape[1])
        x = x + _dot(a, w_ref[...])
    o_ref[...] = x
    x = o_ref[...]
    h = _rms(x, nw_ref[...]).astype(BF16)
    acc = x
    for c0, c1 in _col_chunks(wg_ref.shape[1], chunk):
        g = _dot(h, wg_ref[:, c0:c1])
        u = _dot(h, wu_ref[:, c0:c1])
        a = (_silu(g) * u).astype(BF16)
        acc = acc + _dot(a, wd_ref[c0:c1, :])
    o_ref[...] = acc


def mix_ffn_residual(x, acts, weights, norm_w, wg, wu, wd, *, tm=512, chunk=256):
    t, d = x.shape
    n_in = len(acts)
    row = pl.BlockSpec((tm, d), lambda i: (i, 0))
    in_specs = [row]
    in_specs += [pl.BlockSpec((tm, a.shape[1]), lambda i: (i, 0)) for a in acts]
    in_specs += [_const_spec(w.shape) for w in weights]
    in_specs += [_const_spec((1, d)), _const_spec(wg.shape), _const_spec(wu.shape), _const_spec(wd.shape)]
    return pl.pallas_call(
        functools.partial(_mix_ffn_kernel, n_in=n_in, chunk=chunk),
        grid=(t // tm,),
        in_specs=in_specs,
        out_specs=row,
        out_shape=jax.ShapeDtypeStruct((t, d), F32),
        compiler_params=_cparams("parallel"),
        name="mix_ffn_residual",
    )(x, *acts, *weights, norm_w.reshape(1, d), wg, wu, wd)


_COL_Z = 0
_COL_V = _COL_Z + D_SSM
_COL_G = _COL_V + GLA_VAL_DIM
_COL_XBC = _COL_G + GLA_VAL_DIM
_COL_Q = _COL_XBC + CONV_DIM
_COL_K = _COL_Q + GLA_KEY_DIM
_COL_GKL = _COL_K + GLA_KEY_DIM
_PROJ_COLS = _COL_GKL + LANES
_CONV_HALO = 8
_SSD_STEP_CHUNKS = 4


def _ssd_kernel(z_ref, xs_ref, bc_ref, dt_ref, cw_ref, cb_ref, dtb_ref, nega_ref, dskip_ref, nw_ref,
                tril_ref, exp_ref, rep_ref, o_ref, xpad_ref, state_ref):
    q = SSD_CHUNK
    p = SSD_HEAD_DIM
    n = SSD_STATE
    gw = SSD_HPG * p

    @pl.when(pl.program_id(1) == 0)
    def _():
        xpad_ref[0:_CONV_HALO, :] = jnp.zeros((_CONV_HALO, CONV_DIM), F32)
        state_ref[...] = jnp.zeros_like(state_ref)

    row = lax.broadcasted_iota(jnp.int32, (q, q), 0)
    col = lax.broadcasted_iota(jnp.int32, (q, q), 1)
    causal = row >= col
    lane = lax.broadcasted_iota(jnp.int32, (q, 2 * p), 1)
    for sub in range(z_ref.shape[0] // q):
        rs = slice(sub * q, (sub + 1) * q)
        xpad_ref[_CONV_HALO:, 0:D_SSM] = xs_ref[rs, :].astype(F32)
        xpad_ref[_CONV_HALO:, D_SSM:] = bc_ref[rs, :].astype(F32)
        conv = cb_ref[...]
        for j in range(CONV_WIDTH):
            off = _CONV_HALO - (CONV_WIDTH - 1) + j
            conv = conv + cw_ref[j:j + 1, :] * xpad_ref[off:off + q, :]
        xpad_ref[0:_CONV_HALO, :] = xpad_ref[q:q + _CONV_HALO, :]
        xbc = _silu(conv)

        dt = _softplus(dt_ref[rs, :] + dtb_ref[...])
        a = dt * nega_ref[...]
        acum = _sel_dot(tril_ref[...], a)
        acum_t = acum.T
        dt_x = _dot_sel(dt, exp_ref[...])
        acum_x = _dot_sel(acum, exp_ref[...])
        acum_rep = _dot_sel(acum, rep_ref[...])
        a_last_x = acum_x[q - 1:q, :]
        xs = xbc[:, 0:D_SSM]
        xdt = xs * dt_x
        xdt_bf = xdt.astype(BF16)
        w_end = (xdt * jnp.exp(a_last_x - acum_x)).astype(BF16)
        exp_a = jnp.exp(acum_x)
        chunk_decay = jnp.exp(a_last_x)
        y_skip = xs * dskip_ref[...]

        y_groups = []
        for g in range(SSD_GROUPS):
            gs = slice(g * gw, (g + 1) * gw)
            b_g = xbc[:, D_SSM + g * n:D_SSM + (g + 1) * n]
            c_g = xbc[:, D_SSM + (SSD_GROUPS + g) * n:D_SSM + (SSD_GROUPS + g + 1) * n]
            b_bf = b_g.astype(BF16)
            c_bf = c_g.astype(BF16)
            cb = _dot_nt(c_bf, b_bf)
            s_g = state_ref[g]
            y_off = _dot(c_bf, s_g.astype(BF16)) * exp_a[:, gs]
            y_pairs = []
            for k in range(0, SSD_HPG, 2):
                lhs = []
                for hk in (g * SSD_HPG + k, g * SSD_HPG + k + 1):
                    a_col = acum_rep[:, hk * q:(hk + 1) * q]
                    a_row = acum_t[hk:hk + 1, :]
                    decay = jnp.where(causal, jnp.exp(jnp.minimum(a_col - a_row, 0.0)), 0.0)
                    lhs.append((cb * decay).astype(BF16))
                hk0 = g * SSD_HPG + k
                x_pair = xdt_bf[:, hk0 * p:(hk0 + 2) * p]
                zero = jnp.zeros_like(x_pair)
                rhs = jnp.concatenate([jnp.where(lane < p, x_pair, zero), jnp.where(lane >= p, x_pair, zero)], axis=0)
                y_pairs.append(_dot(jnp.concatenate(lhs, axis=1), rhs))
            state_ref[g] = s_g * chunk_decay[:, gs] + _dot_tn(b_bf, w_end[:, gs])
            y_g = jnp.concatenate(y_pairs, axis=1) + y_off + y_skip[:, gs]
            y_g = y_g * _silu(z_ref[rs, gs].astype(F32))
            y_groups.append(_rms(y_g, nw_ref[:, gs]))
        o_ref[rs, :] = jnp.concatenate(y_groups, axis=1).astype(o_ref.dtype)


def ssd_scan(proj, dt_raw, conv_w, conv_b, dt_bias, a_log, d_skip, norm_w, *, batch):
    t = proj.shape[0]
    q = SSD_CHUNK
    br = _SSD_STEP_CHUNKS * q
    nc = t // batch // br

    def pad_heads(v):
        return jnp.pad(v.reshape(1, SSD_HEADS), ((0, 0), (0, LANES - SSD_HEADS)))

    tril = jnp.tril(jnp.ones((q, q), F32)).astype(BF16)
    head = jnp.arange(LANES)[:, None]
    expand = (head == jnp.arange(D_SSM)[None, :] // SSD_HEAD_DIM).astype(BF16)
    replicate = (head == jnp.arange(SSD_HEADS * q)[None, :] // q).astype(BF16)
    rows = lambda b, c: b * nc + c
    return pl.pallas_call(
        _ssd_kernel,
        grid=(batch, nc),
        in_specs=[
            pl.BlockSpec((br, D_SSM), lambda b, c: (rows(b, c), _COL_Z // D_SSM)),
            pl.BlockSpec((br, D_SSM), lambda b, c: (rows(b, c), _COL_XBC // D_SSM)),
            pl.BlockSpec((br, CONV_DIM - D_SSM), lambda b, c: (rows(b, c), (_COL_XBC + D_SSM) // (CONV_DIM - D_SSM))),
            pl.BlockSpec((br, LANES), lambda b, c: (rows(b, c), 0)),
            _const_spec((CONV_WIDTH, CONV_DIM)),
            _const_spec((1, CONV_DIM)),
            _const_spec((1, LANES)),
            _const_spec((1, LANES)),
            _const_spec((1, D_SSM)),
            _const_spec((1, D_SSM)),
            _const_spec((q, q)),
            _const_spec(expand.shape),
            _const_spec(replicate.shape),
        ],
        out_specs=pl.BlockSpec((br, D_SSM), lambda b, c: (rows(b, c), 0)),
        out_shape=jax.ShapeDtypeStruct((t, D_SSM), BF16),
        scratch_shapes=[
            pltpu.VMEM((q + _CONV_HALO, CONV_DIM), F32),
            pltpu.VMEM((SSD_GROUPS, SSD_STATE, SSD_HPG * SSD_HEAD_DIM), F32),
        ],
        compiler_params=_cparams("parallel", "arbitrary"),
        name="ssd_scan",
    )(proj, proj, proj, dt_raw, conv_w, conv_b.reshape(1, CONV_DIM), pad_heads(dt_bias),
      pad_heads(-jnp.exp(a_log)), jnp.repeat(d_skip, SSD_HEAD_DIM).reshape(1, D_SSM), norm_w.reshape(1, D_SSM),
      tril, expand, replicate)


def _gla_kernel(q_ref, k_ref, v_ref, g_ref, gkl_ref, w2_ref, gkb_ref, nw_ref, tril_ref, o_ref, state_ref,
                *, block):
    c = GLA_CHUNK
    dk, dv = GLA_HEAD_K, GLA_HEAD_V
    scale = dk ** -0.5

    @pl.when(pl.program_id(1) == 0)
    def _():
        state_ref[...] = jnp.zeros_like(state_ref)

    gk = -_softplus(-(_dot(gkl_ref[...], w2_ref[...]) + gkb_ref[...])) * (1.0 / GLA_GATE_NORM)
    row = lax.broadcasted_iota(jnp.int32, (c, c), 0)
    col = lax.broadcasted_iota(jnp.int32, (c, c), 1)
    causal = row >= col
    for j in range(block // c):
        r0, r1 = j * c, (j + 1) * c
        gcum_all = _sel_dot(tril_ref[...], gk[r0:r1])
        for h in range(GLA_HEADS):
            ks, vs = slice(h * dk, (h + 1) * dk), slice(h * dv, (h + 1) * dv)
            gcum = gcum_all[:, ks]
            g_last = gcum[c - 1:c, :]
            qc = q_ref[r0:r1, ks].astype(F32) * scale
            kc = k_ref[r0:r1, ks].astype(F32)
            vc = v_ref[r0:r1, vs]
            q_dec = (qc * jnp.exp(gcum)).astype(BF16)
            k_inv = (kc * jnp.exp(-gcum)).astype(BF16)
            k_end = (kc * jnp.exp(g_last - gcum)).astype(BF16)
            scores = jnp.where(causal, _dot_nt(q_dec, k_inv), 0.0)
            s_t = state_ref[h]
            o = _dot(scores.astype(BF16), vc) + _dot_nt(q_dec, s_t.astype(BF16))
            state_ref[h] = s_t * jnp.exp(g_last) + _dot_tn(vc, k_end)
            o = _rms(o, nw_ref[...]) * _silu(g_ref[r0:r1, vs].astype(F32))
            o_ref[r0:r1, vs] = o.astype(o_ref.dtype)


def gla_scan(proj, gk_w2, gk_b, norm_w, *, batch, block=512):
    t = proj.shape[0]
    nb = t // batch // block
    kd, vd = GLA_KEY_DIM, GLA_VAL_DIM
    w2 = jnp.pad(gk_w2, ((0, LANES - GLA_GATE_RANK), (0, 0))).astype(BF16)
    tril = jnp.tril(jnp.ones((GLA_CHUNK, GLA_CHUNK), F32)).astype(BF16)
    rows = lambda b, i: b * nb + i
    return pl.pallas_call(
        functools.partial(_gla_kernel, block=block),
        grid=(batch, nb),
        in_specs=[
            pl.BlockSpec((block, kd), lambda b, i: (rows(b, i), _COL_Q // kd)),
            pl.BlockSpec((block, kd), lambda b, i: (rows(b, i), _COL_K // kd)),
            pl.BlockSpec((block, vd), lambda b, i: (rows(b, i), _COL_V // vd)),
            pl.BlockSpec((block, vd), lambda b, i: (rows(b, i), _COL_G // vd)),
            pl.BlockSpec((block, LANES), lambda b, i: (rows(b, i), _COL_GKL // LANES)),
            _const_spec((LANES, kd)),
            _const_spec((1, kd)),
            _const_spec((1, GLA_HEAD_V)),
            _const_spec((GLA_CHUNK, GLA_CHUNK)),
        ],
        out_specs=pl.BlockSpec((block, vd), lambda b, i: (rows(b, i), 0)),
        out_shape=jax.ShapeDtypeStruct((t, vd), BF16),
        scratch_shapes=[pltpu.VMEM((GLA_HEADS, GLA_HEAD_V, GLA_HEAD_K), F32)],
        compiler_params=_cparams("parallel", "arbitrary"),
        name="gla_scan",
    )(proj, proj, proj, proj, proj, w2, gk_b.reshape(1, kd), norm_w.reshape(1, GLA_HEAD_V), tril)


_DQ_CQ = 0
_DQ_CKV = 512
_DQ_KRA = 768
_DQ_KRB = 896
_DQ_COLS = 1024
_QKV_COLS = MLA_HEADS * HEAD_PAD
V7X_MXU_COLS = 256
_HEAD_GROUP = V7X_MXU_COLS // HEAD_PAD
_GROUP_COLS = _HEAD_GROUP * HEAD_PAD


def _mla_qkv_kernel(x_ref, nw_ref, wd_ref, qlw_ref, wa_ref, wb_ref, ga_ref, gb_ref, klw_ref, wk_ref, wv_ref,
                    gk_ref, gra_ref, grb_ref, cos_ref, sin_ref, pm_ref, ones_ref, q_ref, k_ref, v_ref, c_ref,
                    *, scale):
    h = _rms(x_ref[...], nw_ref[...]).astype(BF16)
    for c0, c1 in _col_chunks(_DQ_COLS, 512):
        c_ref[:, c0:c1] = _dot(h, wd_ref[:, c0:c1])
    cos = cos_ref[...]
    sin = sin_ref[...]
    pm = pm_ref[...]

    cq = _rms(c_ref[:, _DQ_CQ:_DQ_CQ + MLA_Q_RANK], qlw_ref[...]).astype(BF16)
    fa = jnp.concatenate([ga_ref[...] * cos * scale] * _HEAD_GROUP, axis=1)
    fb = jnp.concatenate([gb_ref[...] * sin * scale] * _HEAD_GROUP, axis=1)
    for g in range(MLA_HEADS // _HEAD_GROUP):
        sl = slice(g * _GROUP_COLS, (g + 1) * _GROUP_COLS)
        a = _dot(cq, wa_ref[:, sl])
        b = _dot(cq, wb_ref[:, sl])
        ms = _dot((a * a).astype(BF16), pm)
        q_ref[:, sl] = ((a * fa + b * fb) * lax.rsqrt(ms + NORM_EPS)).astype(q_ref.dtype)

    ckv = _rms(c_ref[:, _DQ_CKV:_DQ_CKV + MLA_KV_RANK], klw_ref[...]).astype(BF16)
    kra = c_ref[:, _DQ_KRA:_DQ_KRA + HEAD_PAD]
    krb = c_ref[:, _DQ_KRB:_DQ_KRB + HEAD_PAD]
    r_rope = lax.rsqrt(jnp.sum(kra * kra, axis=-1, keepdims=True) * (1.0 / MLA_ROPE) + NORM_EPS)
    k_rope = (kra * (gra_ref[...] * cos) + krb * (grb_ref[...] * sin)) * r_rope
    k_rope = jnp.concatenate([k_rope] * _HEAD_GROUP, axis=1)
    gk = jnp.concatenate([gk_ref[...]] * _HEAD_GROUP, axis=1)
    ones = jnp.concatenate([ones_ref[...]] * _HEAD_GROUP, axis=1)
    for g in range(MLA_HEADS // _HEAD_GROUP):
        sl = slice(g * _GROUP_COLS, (g + 1) * _GROUP_COLS)
        kn = _dot(ckv, wk_ref[:, sl])
        ms = _dot((kn * kn).astype(BF16), pm)
        k_ref[:, sl] = (kn * gk * lax.rsqrt(ms + NORM_EPS) + k_rope).astype(k_ref.dtype)
        v_ref[:, sl] = (_dot(ckv, wv_ref[:, sl]) + ones).astype(v_ref.dtype)


def _part_mean_matrix():
    lane = jnp.arange(_GROUP_COLS)
    head, pos = lane // HEAD_PAD, lane % HEAD_PAD
    part = jnp.where(pos < MLA_NOPE, 0, jnp.where(pos < MLA_NOPE + MLA_ROPE, 1, 2))
    size = jnp.where(part == 0, MLA_NOPE, MLA_ROPE).astype(F32)
    same = (head[:, None] == head[None, :]) & (part[:, None] == part[None, :]) & (part[:, None] < 2)
    return (same.astype(F32) / size[None, :]).astype(BF16)


def _rope_tables(seqlen, batch):
    pos = jnp.arange(seqlen, dtype=F32)
    inv = 1.0 / (ROPE_THETA ** (jnp.arange(0, MLA_ROPE, 2, dtype=F32) / MLA_ROPE))
    ang = pos[:, None] * inv[None, :]
    cos, sin = jnp.cos(ang), jnp.sin(ang)
    pad = jnp.zeros((seqlen, HEAD_PAD - MLA_NOPE - MLA_ROPE), F32)
    cos_t = jnp.concatenate([jnp.ones((seqlen, MLA_NOPE), F32), cos, cos, pad], axis=1)
    sin_t = jnp.concatenate([jnp.zeros((seqlen, MLA_NOPE), F32), -sin, sin, pad], axis=1)
    return jnp.tile(cos_t, (batch, 1)), jnp.tile(sin_t, (batch, 1))


def _swap_halves(w):
    half = w.shape[-1] // 2
    return jnp.concatenate([w[..., half:], w[..., :half]], axis=-1)


def _rope_lanes(w):
    pad = [(0, 0)] * (w.ndim - 1) + [(MLA_NOPE, HEAD_PAD - MLA_NOPE - MLA_ROPE)]
    return jnp.pad(w, pad)


def mla_qkv(x, norm_w, w_dqkv, q_lora_norm, w_uq, kv_lora_norm, w_ukv, q_nope_norm, q_rope_norm,
            k_nope_norm, k_rope_norm, cos_t, sin_t, *, tm=512):
    t, d = x.shape
    pad_rope = HEAD_PAD - MLA_NOPE - MLA_ROPE
    w_cq = w_dqkv[:, :MLA_Q_RANK]
    w_ckv = w_dqkv[:, MLA_Q_RANK:MLA_Q_RANK + MLA_KV_RANK]
    w_kr = w_dqkv[:, MLA_Q_RANK + MLA_KV_RANK:]
    wd = jnp.concatenate(
        [w_cq, jnp.zeros((d, _DQ_CKV - MLA_Q_RANK), F32), w_ckv, _rope_lanes(w_kr),
         _rope_lanes(_swap_halves(w_kr))], axis=1).astype(BF16)
    wq = w_uq.reshape(MLA_Q_RANK, MLA_HEADS, MLA_NOPE + MLA_ROPE)
    wq_nope, wq_rope = wq[..., :MLA_NOPE], wq[..., MLA_NOPE:]
    wa = jnp.concatenate([wq_nope, wq_rope, jnp.zeros((MLA_Q_RANK, MLA_HEADS, pad_rope), F32)], axis=-1)
    wa = wa.reshape(MLA_Q_RANK, _QKV_COLS).astype(BF16)
    wb = _rope_lanes(_swap_halves(wq_rope)).reshape(MLA_Q_RANK, _QKV_COLS).astype(BF16)
    ga = jnp.concatenate([q_nope_norm, q_rope_norm, jnp.zeros((pad_rope,), F32)]).reshape(1, HEAD_PAD)
    gb = _rope_lanes(_swap_halves(q_rope_norm)).reshape(1, HEAD_PAD)
    wkv = w_ukv.reshape(MLA_KV_RANK, MLA_HEADS, MLA_NOPE + MLA_V)
    wk = jnp.pad(wkv[..., :MLA_NOPE], [(0, 0), (0, 0), (0, HEAD_PAD - MLA_NOPE)])
    wv = jnp.pad(wkv[..., MLA_NOPE:], [(0, 0), (0, 0), (0, HEAD_PAD - MLA_V)])
    wk = wk.reshape(MLA_KV_RANK, _QKV_COLS).astype(BF16)
    wv = wv.reshape(MLA_KV_RANK, _QKV_COLS).astype(BF16)
    gk = jnp.pad(k_nope_norm, (0, HEAD_PAD - MLA_NOPE)).reshape(1, HEAD_PAD)
    gra = _rope_lanes(k_rope_norm).reshape(1, HEAD_PAD)
    grb = _rope_lanes(_swap_halves(k_rope_norm)).reshape(1, HEAD_PAD)
    ones = (jnp.arange(HEAD_PAD) >= MLA_V).astype(F32).reshape(1, HEAD_PAD)
    pm = _part_mean_matrix()
    row = lambda width: pl.BlockSpec((tm, width), lambda i: (i, 0))
    vec = lambda n: _const_spec((1, n))
    return pl.pallas_call(
        functools.partial(_mla_qkv_kernel, scale=(MLA_NOPE + MLA_ROPE) ** -0.5 * math.log2(math.e)),
        grid=(t // tm,),
        in_specs=[
            row(d), vec(d), _const_spec(wd.shape),
            vec(MLA_Q_RANK), _const_spec(wa.shape), _const_spec(wb.shape), vec(HEAD_PAD), vec(HEAD_PAD),
            vec(MLA_KV_RANK), _const_spec(wk.shape), _const_spec(wv.shape), vec(HEAD_PAD), vec(HEAD_PAD), vec(HEAD_PAD),
            row(HEAD_PAD), row(HEAD_PAD), _const_spec(pm.shape), vec(HEAD_PAD),
        ],
        out_specs=[row(_QKV_COLS)] * 3,
        out_shape=[jax.ShapeDtypeStruct((t, _QKV_COLS), BF16)] * 3,
        scratch_shapes=[pltpu.VMEM((tm, _DQ_COLS), F32)],
        compiler_params=_cparams("parallel"),
        name="mla_qkv",
    )(x, norm_w.reshape(1, d), wd, q_lora_norm.reshape(1, MLA_Q_RANK), wa, wb, ga, gb,
      kv_lora_norm.reshape(1, MLA_KV_RANK), wk, wv, gk, gra, grb, cos_t, sin_t, pm, ones)


_ATTN_UNROLL = 4


def _attn_kernel(q_ref, k_ref, v_ref, o_ref, s_ref, p_ref, mx_ref, m_ref, alpha_ref, acc_ref, *, tq, tk):
    qi = pl.program_id(2)

    def rows(j):
        return pl.ds(pl.multiple_of(j * tk, tk), tk)

    def stage_a(slot, j, limit=None, r0=0):
        nr = tq - r0
        s = _dot_nt(q_ref[r0:, :], k_ref[rows(j), :])
        if limit is not None:
            rel = (lax.broadcasted_iota(jnp.int32, (nr, tk), 1) - lax.broadcasted_iota(jnp.int32, (nr, tk), 0))
            s = jnp.where(rel <= limit + r0, s, -jnp.inf)
        s_ref[slot, r0:, :] = s
        mx_ref[slot, r0:, :] = jnp.broadcast_to(jnp.max(s, axis=-1, keepdims=True), (nr, LANES))

    def stage_b(slot, r0=0):
        m = m_ref[r0:, :]
        m_new = jnp.maximum(m, mx_ref[slot, r0:, :])
        for c in range(0, tk, LANES):
            pb = jnp.exp2(s_ref[slot, r0:, c:c + LANES] - m_new).astype(BF16)
            p_ref[slot, r0 // 2:, c:c + LANES] = pltpu.bitcast(pb, jnp.uint32)
        m_ref[r0:, :] = m_new
        alpha_ref[r0:, :] = jnp.exp2(m - m_new)

    def stage_c(slot, j, r0=0):
        p = pltpu.bitcast(p_ref[slot, r0 // 2:, :], BF16)
        acc_ref[r0:, :] = alpha_ref[r0:, :] * acc_ref[r0:, :] + _dot(p, v_ref[rows(j), :])

    def trip(ii, limit):
        j = 2 * ii
        stage_a(1, j + 1)
        stage_c(1, jnp.maximum(j - 1, 0))
        stage_b(0)
        stage_a(0, j + 2, limit=limit)
        stage_c(0, j)
        stage_b(1)

    p_ref[1] = jnp.zeros((tq // 2, tk), jnp.uint32)
    m_ref[...] = jnp.full((tq, LANES), -jnp.inf, F32)
    alpha_ref[...] = jnp.ones((tq, LANES), F32)
    acc_ref[...] = jnp.zeros((tq, HEAD_PAD), F32)
    stage_a(0, 0, limit=qi * tq)

    n_plain = jnp.maximum(qi - 1, 0)
    n_body = jnp.right_shift(n_plain, _ATTN_UNROLL.bit_length() - 1)

    @pl.loop(0, n_body)
    def _(ib):
        for u in range(_ATTN_UNROLL):
            trip(_ATTN_UNROLL * ib + u, None)

    for r in range(_ATTN_UNROLL - 1):
        @pl.when(n_plain - _ATTN_UNROLL * n_body > r)
        def _():
            trip(_ATTN_UNROLL * n_body + r, None)

    @pl.when(qi > 0)
    def _():
        trip(qi - 1, 0)

    jd = 2 * qi
    stage_a(1, jd + 1, limit=-tk, r0=tk)
    stage_c(1, jnp.maximum(jd - 1, 0))
    stage_b(0)
    stage_c(0, jd)
    stage_b(1, r0=tk)
    stage_c(1, jd + 1, r0=tk)
    acc = acc_ref[...]
    o_ref[...] = (acc / acc[:, MLA_V:MLA_V + 1]).astype(o_ref.dtype)


def flash_attention(q, k, v, *, batch, tk=512):
    t = q.shape[0]
    seqlen = t // batch
    tq = 2 * tk
    nq = seqlen // tq
    return pl.pallas_call(
        functools.partial(_attn_kernel, tq=tq, tk=tk),
        grid=(batch, MLA_HEADS, nq),
        in_specs=[
            pl.BlockSpec((tq, HEAD_PAD), lambda b, h, i: (b * nq + i, h)),
            pl.BlockSpec((seqlen, HEAD_PAD), lambda b, h, i: (b, h)),
            pl.BlockSpec((seqlen, HEAD_PAD), lambda b, h, i: (b, h)),
        ],
        out_specs=pl.BlockSpec((tq, HEAD_PAD), lambda b, h, i: (b * nq + i, h)),
        out_shape=jax.ShapeDtypeStruct((t, _QKV_COLS), BF16),
        scratch_shapes=[
            pltpu.VMEM((2, tq, tk), F32),
            pltpu.VMEM((2, tq // 2, tk), jnp.uint32),
            pltpu.VMEM((2, tq, LANES), F32),
            pltpu.VMEM((tq, LANES), F32),
            pltpu.VMEM((tq, LANES), F32),
            pltpu.VMEM((tq, HEAD_PAD), F32),
        ],
        compiler_params=_cparams("parallel", "parallel", "arbitrary"),
        name="mla_attention",
    )(q, k, v)


def _even_mixer(x, batch, norm_w, w_in, conv_w, conv_b, dt_bias, a_log, d_skip, ssd_norm,
                gk_w2, gk_b, gla_norm, w_out):
    sizes = (D_SSM, CONV_DIM, SSD_HEADS, GLA_KEY_DIM, GLA_KEY_DIM, GLA_VAL_DIM, GLA_VAL_DIM, GLA_GATE_RANK)
    offs = [0]
    for s in sizes:
        offs.append(offs[-1] + s)
    seg = [w_in[:, offs[i]:offs[i + 1]] for i in range(len(sizes))]
    w_z, w_xbc, w_dt, w_q, w_k, w_v, w_g, w_gkl = seg
    w_main = jnp.concatenate(
        [w_z, w_v, w_g, w_xbc, w_q, w_k, jnp.pad(w_gkl, ((0, 0), (0, LANES - GLA_GATE_RANK)))], axis=1).astype(BF16)
    w_dtp = jnp.pad(w_dt, ((0, 0), (0, LANES - SSD_HEADS))).astype(BF16)
    proj, dt_raw = norm_proj(x, norm_w, [w_main, w_dtp], [BF16, F32])
    y_ssd = ssd_scan(proj, dt_raw, conv_w, conv_b, dt_bias, a_log, d_skip, ssd_norm, batch=batch)
    y_gla = gla_scan(proj, gk_w2, gk_b, gla_norm, batch=batch)
    w_o = w_out.astype(BF16)
    return [y_ssd, y_gla], [w_o[:D_SSM], w_o[D_SSM:]]


def _odd_mixer(x, batch, cos_t, sin_t, norm_w, w_dqkv, q_lora_norm, w_uq, kv_lora_norm, w_ukv,
               q_nope_norm, q_rope_norm, k_nope_norm, k_rope_norm, w_o):
    q, k, v = mla_qkv(x, norm_w, w_dqkv, q_lora_norm, w_uq, kv_lora_norm, w_ukv, q_nope_norm, q_rope_norm,
                      k_nope_norm, k_rope_norm, cos_t, sin_t)
    o = flash_attention(q, k, v, batch=batch)
    return [o], [w_o.astype(BF16)]


def kernel(x, mix_norm_even, w_in_even, conv_w, conv_b, dt_bias, a_log, d_skip, ssd_norm, gla_gk_w2, gla_gk_b, gla_norm, w_out_even, mix_norm_odd, w_dqkv, q_lora_norm, w_uq, kv_lora_norm, w_ukv, q_nope_norm, q_rope_norm, k_nope_norm, k_rope_norm, w_o_mla, ffn_norm, w_gate, w_up, w_down):
    batch, seqlen, d = x.shape
    depth = ffn_norm.shape[0]
    cos_t, sin_t = _rope_tables(seqlen, batch)
    x = x.reshape(batch * seqlen, d)
    for i in range(depth):
        j = i // 2
        if i % 2 == 0:
            acts, w_outs = _even_mixer(x, batch, mix_norm_even[j], w_in_even[j], conv_w[j], conv_b[j], dt_bias[j],
                                       a_log[j], d_skip[j], ssd_norm[j], gla_gk_w2[j], gla_gk_b[j], gla_norm[j],
                                       w_out_even[j])
        else:
            acts, w_outs = _odd_mixer(x, batch, cos_t, sin_t, mix_norm_odd[j], w_dqkv[j], q_lora_norm[j], w_uq[j],
                                      kv_lora_norm[j], w_ukv[j], q_nope_norm[j], q_rope_norm[j], k_nope_norm[j],
                                      k_rope_norm[j], w_o_mla[j])
        x = mix_ffn_residual(x, acts, w_outs, ffn_norm[i], w_gate[i].astype(BF16), w_up[i].astype(BF16),
                             w_down[i].astype(BF16))
    return x.reshape(batch, seqlen, d)
```

```python
import functools
import math

import jax
import jax.numpy as jnp
from jax import lax
from jax.experimental import pallas as pl
from jax.experimental.pallas import tpu as pltpu

F32 = jnp.float32
BF16 = jnp.bfloat16

D_MODEL = 1024
NORM_EPS = 1e-6

SSD_HEAD_DIM = 64
SSD_HEADS = 16
SSD_GROUPS = 2
SSD_HPG = SSD_HEADS // SSD_GROUPS
SSD_STATE = 128
SSD_CHUNK = 128
CONV_WIDTH = 4
D_SSM = SSD_HEADS * SSD_HEAD_DIM
CONV_DIM = D_SSM + 2 * SSD_GROUPS * SSD_STATE

GLA_HEADS = 4
GLA_KEY_DIM = D_MODEL // 2
GLA_VAL_DIM = D_MODEL
GLA_HEAD_K = GLA_KEY_DIM // GLA_HEADS
GLA_HEAD_V = GLA_VAL_DIM // GLA_HEADS
GLA_GATE_RANK = 16
GLA_GATE_NORM = 16.0
GLA_CHUNK = 64

MLA_HEADS = 16
MLA_NOPE = 64
MLA_ROPE = 32
MLA_V = 64
MLA_Q_RANK = 384
MLA_KV_RANK = 256
ROPE_THETA = 10000.0

LANES = 128
HEAD_PAD = 128
V7X_VMEM_BYTES = 64 * 1024 * 1024
VMEM_LIMIT = V7X_VMEM_BYTES - 8 * 1024 * 1024


def _cparams(*sem):
    return pltpu.CompilerParams(dimension_semantics=sem, vmem_limit_bytes=VMEM_LIMIT)


def _dot(a, b):
    return jnp.dot(a, b, preferred_element_type=F32)


def _dot_nt(a, b):
    return lax.dot_general(a, b, (((1,), (1,)), ((), ())), preferred_element_type=F32)


def _dot_tn(a, b):
    return lax.dot_general(a, b, (((0,), (0,)), ((), ())), preferred_element_type=F32)


def _split3(x):
    hi = x.astype(BF16)
    r1 = x - hi.astype(F32)
    mid = r1.astype(BF16)
    lo = (r1 - mid.astype(F32)).astype(BF16)
    return hi, mid, lo


def _dot_sel(x, m):
    hi, mid, lo = _split3(x)
    return _dot(hi, m) + _dot(mid, m) + _dot(lo, m)


def _sel_dot(m, x):
    hi, mid, lo = _split3(x)
    return _dot(m, hi) + _dot(m, mid) + _dot(m, lo)


def _rms(x, w):
    return x * lax.rsqrt(jnp.mean(x * x, axis=-1, keepdims=True) + NORM_EPS) * w


def _silu(x):
    return x * jax.nn.sigmoid(x)


def _softplus(x):
    return jnp.maximum(x, 0.0) + jnp.log1p(jnp.exp(-jnp.abs(x)))


def _col_chunks(n, step):
    return [(c, min(c + step, n)) for c in range(0, n, step)]


def _const_spec(shape):
    return pl.BlockSpec(shape, lambda *_: (0,) * len(shape), pipeline_mode=pl.Buffered(1))


def _norm_proj_kernel(*refs, n_out, chunk):
    x_ref, nw_ref = refs[0], refs[1]
    w_refs = refs[2:2 + n_out]
    o_refs = refs[2 + n_out:]
    h = _rms(x_ref[...].astype(F32), nw_ref[...]).astype(BF16)
    for w_ref, o_ref in zip(w_refs, o_refs):
        for c0, c1 in _col_chunks(w_ref.shape[1], chunk):
            o_ref[:, c0:c1] = _dot(h, w_ref[:, c0:c1]).astype(o_ref.dtype)


def norm_proj(x, norm_w, weights, out_dtypes, *, tm=512, chunk=512):
    t, k = x.shape
    n_out = len(weights)
    in_specs = [pl.BlockSpec((tm, k), lambda i: (i, 0)), _const_spec((1, k))]
    in_specs += [_const_spec(w.shape) for w in weights]
    out_specs = [pl.BlockSpec((tm, w.shape[1]), lambda i: (i, 0)) for w in weights]
    out_shape = [jax.ShapeDtypeStruct((t, w.shape[1]), dt) for w, dt in zip(weights, out_dtypes)]
    return pl.pallas_call(
        functools.partial(_norm_proj_kernel, n_out=n_out, chunk=chunk),
        grid=(t // tm,),
        in_specs=in_specs,
        out_specs=out_specs,
        out_shape=out_shape,
        compiler_params=_cparams("parallel"),
        name="norm_proj",
    )(x, norm_w.reshape(1, k), *weights)


def _drop_group_padding(a, keep):
    return jnp.concatenate([a[:, g * LANES:g * LANES + keep] for g in range(a.shape[1] // LANES)], axis=1)


def _mix_ffn_kernel(*refs, n_in, chunk):
    x_ref = refs[0]
    a_refs = refs[1:1 + n_in]
    w_refs = refs[1 + n_in:1 + 2 * n_in]
    nw_ref, wg_ref, wu_ref, wd_ref, o_ref = refs[1 + 2 * n_in:]
    x = x_ref[...]
    for a_ref, w_ref in zip(a_refs, w_refs):
        a = a_ref[...]
        if a.shape[1] != w_ref.shape[0]:
            a = _drop_group_padding(a, w_ref.shape[0] * LANES // a.shape[1])
        x = x + _dot(a, w_ref[...])
    o_ref[...] = x
    x = o_ref[...]
    h = _rms(x, nw_ref[...]).astype(BF16)
    acc = x
    for c0, c1 in _col_chunks(wg_ref.shape[1], chunk):
        g = _dot(h, wg_ref[:, c0:c1])
        u = _dot(h, wu_ref[:, c0:c1])
        a = (_silu(g) * u).astype(BF16)
        acc = acc + _dot(a, wd_ref[c0:c1, :])
    o_ref[...] = acc


def mix_ffn_residual(x, acts, weights, norm_w, wg, wu, wd, *, tm=512, chunk=256):
    t, d = x.shape
    n_in = len(acts)
    row = pl.BlockSpec((tm, d), lambda i: (i, 0))
    in_specs = [row]
    in_specs += [pl.BlockSpec((tm, a.shape[1]), lambda i: (i, 0)) for a in acts]
    in_specs += [_const_spec(w.shape) for w in weights]
    in_specs += [_const_spec((1, d)), _const_spec(wg.shape), _const_spec(wu.shape), _const_spec(wd.shape)]
    return pl.pallas_call(
        functools.partial(_mix_ffn_kernel, n_in=n_in, chunk=chunk),
        grid=(t // tm,),
        in_specs=in_specs,
        out_specs=row,
        out_shape=jax.ShapeDtypeStruct((t, d), F32),
        compiler_params=_cparams("parallel"),
        name="mix_ffn_residual",
    )(x, *acts, *weights, norm_w.reshape(1, d), wg, wu, wd)


_COL_Z = 0
_COL_V = _COL_Z + D_SSM
_COL_G = _COL_V + GLA_VAL_DIM
_COL_XBC = _COL_G + GLA_VAL_DIM
_COL_Q = _COL_XBC + CONV_DIM
_COL_K = _COL_Q + GLA_KEY_DIM
_COL_GKL = _COL_K + GLA_KEY_DIM
_PROJ_COLS = _COL_GKL + LANES
_CONV_HALO = 8
_SSD_STEP_CHUNKS = 1


def _ssd_kernel(z_ref, xs_ref, bc_ref, dt_ref, cw_ref, cb_ref, dtb_ref, nega_ref, dskip_ref, nw_ref,
                tril_ref, exp_ref, rep_ref, o_ref, xpad_ref, state_ref):
    q = SSD_CHUNK
    p = SSD_HEAD_DIM
    n = SSD_STATE
    gw = SSD_HPG * p

    @pl.when(pl.program_id(1) == 0)
    def _():
        xpad_ref[0:_CONV_HALO, :] = jnp.zeros((_CONV_HALO, CONV_DIM), F32)
        state_ref[...] = jnp.zeros_like(state_ref)

    row = lax.broadcasted_iota(jnp.int32, (q, q), 0)
    col = lax.broadcasted_iota(jnp.int32, (q, q), 1)
    causal = row >= col
    lane = lax.broadcasted_iota(jnp.int32, (q, 2 * p), 1)
    for sub in range(z_ref.shape[0] // q):
        rs = slice(sub * q, (sub + 1) * q)
        xpad_ref[_CONV_HALO:, 0:D_SSM] = xs_ref[rs, :].astype(F32)
        xpad_ref[_CONV_HALO:, D_SSM:] = bc_ref[rs, :].astype(F32)
        conv = cb_ref[...]
        for j in range(CONV_WIDTH):
            off = _CONV_HALO - (CONV_WIDTH - 1) + j
            conv = conv + cw_ref[j:j + 1, :] * xpad_ref[off:off + q, :]
        xpad_ref[0:_CONV_HALO, :] = xpad_ref[q:q + _CONV_HALO, :]
        xbc = _silu(conv)

        dt = _softplus(dt_ref[rs, :] + dtb_ref[...])
        a = dt * nega_ref[...]
        acum = _sel_dot(tril_ref[...], a)
        acum_t = acum.T
        dt_x = _dot_sel(dt, exp_ref[...])
        acum_x = _dot_sel(acum, exp_ref[...])
        acum_rep = _dot_sel(acum, rep_ref[...])
        a_last_x = acum_x[q - 1:q, :]
        xs = xbc[:, 0:D_SSM]
        xdt = xs * dt_x
        xdt_bf = xdt.astype(BF16)
        w_end = (xdt * jnp.exp(a_last_x - acum_x)).astype(BF16)
        exp_a = jnp.exp(acum_x)
        chunk_decay = jnp.exp(a_last_x)
        y_skip = xs * dskip_ref[...]

        y_groups = []
        for g in range(SSD_GROUPS):
            gs = slice(g * gw, (g + 1) * gw)
            b_g = xbc[:, D_SSM + g * n:D_SSM + (g + 1) * n]
            c_g = xbc[:, D_SSM + (SSD_GROUPS + g) * n:D_SSM + (SSD_GROUPS + g + 1) * n]
            b_bf = b_g.astype(BF16)
            c_bf = c_g.astype(BF16)
            cb = _dot_nt(c_bf, b_bf)
            s_g = state_ref[g]
            y_off = _dot(c_bf, s_g.astype(BF16)) * exp_a[:, gs]
            y_pairs = []
            for k in range(0, SSD_HPG, 2):
                lhs = []
                for hk in (g * SSD_HPG + k, g * SSD_HPG + k + 1):
                    a_col = acum_rep[:, hk * q:(hk + 1) * q]
                    a_row = acum_t[hk:hk + 1, :]
                    decay = jnp.where(causal, jnp.exp(jnp.minimum(a_col - a_row, 0.0)), 0.0)
                    lhs.append((cb * decay).astype(BF16))
                hk0 = g * SSD_HPG + k
                x_pair = xdt_bf[:, hk0 * p:(hk0 + 2) * p]
                zero = jnp.zeros_like(x_pair)
                rhs = jnp.concatenate([jnp.where(lane < p, x_pair, zero), jnp.where(lane >= p, x_pair, zero)], axis=0)
                y_pairs.append(_dot(jnp.concatenate(lhs, axis=1), rhs))
            state_ref[g] = s_g * chunk_decay[:, gs] + _dot_tn(b_bf, w_end[:, gs])
            y_g = jnp.concatenate(y_pairs, axis=1) + y_off + y_skip[:, gs]
            y_g = y_g * _silu(z_ref[rs, gs].astype(F32))
            y_groups.append(_rms(y_g, nw_ref[:, gs]))
        o_ref[rs, :] = jnp.concatenate(y_groups, axis=1).astype(o_ref.dtype)


def ssd_scan(proj, dt_raw, conv_w, conv_b, dt_bias, a_log, d_skip, norm_w, *, batch):
    t = proj.shape[0]
    q = SSD_CHUNK
    br = _SSD_STEP_CHUNKS * q
    nc = t // batch // br

    def pad_heads(v):
        return jnp.pad(v.reshape(1, SSD_HEADS), ((0, 0), (0, LANES - SSD_HEADS)))

    tril = jnp.tril(jnp.ones((q, q), F32)).astype(BF16)
    head = jnp.arange(LANES)[:, None]
    expand = (head == jnp.arange(D_SSM)[None, :] // SSD_HEAD_DIM).astype(BF16)
    replicate = (head == jnp.arange(SSD_HEADS * q)[None, :] // q).astype(BF16)
    rows = lambda b, c: b * nc + c
    return pl.pallas_call(
        _ssd_kernel,
        grid=(batch, nc),
        in_specs=[
            pl.BlockSpec((br, D_SSM), lambda b, c: (rows(b, c), _COL_Z // D_SSM)),
            pl.BlockSpec((br, D_SSM), lambda b, c: (rows(b, c), _COL_XBC // D_SSM)),
            pl.BlockSpec((br, CONV_DIM - D_SSM), lambda b, c: (rows(b, c), (_COL_XBC + D_SSM) // (CONV_DIM - D_SSM))),
            pl.BlockSpec((br, LANES), lambda b, c: (rows(b, c), 0)),
            _const_spec((CONV_WIDTH, CONV_DIM)),
            _const_spec((1, CONV_DIM)),
            _const_spec((1, LANES)),
            _const_spec((1, LANES)),
            _const_spec((1, D_SSM)),
            _const_spec((1, D_SSM)),
            _const_spec((q, q)),
            _const_spec(expand.shape),
            _const_spec(replicate.shape),
        ],
        out_specs=pl.BlockSpec((br, D_SSM), lambda b, c: (rows(b, c), 0)),
        out_shape=jax.ShapeDtypeStruct((t, D_SSM), BF16),
        scratch_shapes=[
            pltpu.VMEM((q + _CONV_HALO, CONV_DIM), F32),
            pltpu.VMEM((SSD_GROUPS, SSD_STATE, SSD_HPG * SSD_HEAD_DIM), F32),
        ],
        compiler_params=_cparams("parallel", "arbitrary"),
        name="ssd_scan",
    )(proj, proj, proj, dt_raw, conv_w, conv_b.reshape(1, CONV_DIM), pad_heads(dt_bias),
      pad_heads(-jnp.exp(a_log)), jnp.repeat(d_skip, SSD_HEAD_DIM).reshape(1, D_SSM), norm_w.reshape(1, D_SSM),
      tril, expand, replicate)


def _gla_kernel(q_ref, k_ref, v_ref, g_ref, gkl_ref, w2_ref, gkb_ref, nw_ref, tril_ref, o_ref, state_ref,
                *, block):
    c = GLA_CHUNK
    dk, dv = GLA_HEAD_K, GLA_HEAD_V
    scale = dk ** -0.5

    @pl.when(pl.program_id(1) == 0)
    def _():
        state_ref[...] = jnp.zeros_like(state_ref)

    gk = -_softplus(-(_dot(gkl_ref[...], w2_ref[...]) + gkb_ref[...])) * (1.0 / GLA_GATE_NORM)
    row = lax.broadcasted_iota(jnp.int32, (c, c), 0)
    col = lax.broadcasted_iota(jnp.int32, (c, c), 1)
    causal = row >= col
    for j in range(block // c):
        r0, r1 = j * c, (j + 1) * c
        gcum_all = _sel_dot(tril_ref[...], gk[r0:r1])
        for h in range(GLA_HEADS):
            ks, vs = slice(h * dk, (h + 1) * dk), slice(h * dv, (h + 1) * dv)
            gcum = gcum_all[:, ks]
            g_last = gcum[c - 1:c, :]
            qc = q_ref[r0:r1, ks].astype(F32) * scale
            kc = k_ref[r0:r1, ks].astype(F32)
            vc = v_ref[r0:r1, vs]
            q_dec = (qc * jnp.exp(gcum)).astype(BF16)
            k_inv = (kc * jnp.exp(-gcum)).astype(BF16)
            k_end = (kc * jnp.exp(g_last - gcum)).astype(BF16)
            scores = jnp.where(causal, _dot_nt(q_dec, k_inv), 0.0)
            s_t = state_ref[h]
            o = _dot(scores.astype(BF16), vc) + _dot_nt(q_dec, s_t.astype(BF16))
            state_ref[h] = s_t * jnp.exp(g_last) + _dot_tn(vc, k_end)
            o = _rms(o, nw_ref[...]) * _silu(g_ref[r0:r1, vs].astype(F32))
            o_ref[r0:r1, vs] = o.astype(o_ref.dtype)


def gla_scan(proj, gk_w2, gk_b, norm_w, *, batch, block=512):
    t = proj.shape[0]
    nb = t // batch // block
    kd, vd = GLA_KEY_DIM, GLA_VAL_DIM
    w2 = jnp.pad(gk_w2, ((0, LANES - GLA_GATE_RANK), (0, 0))).astype(BF16)
    tril = jnp.tril(jnp.ones((GLA_CHUNK, GLA_CHUNK), F32)).astype(BF16)
    rows = lambda b, i: b * nb + i
    return pl.pallas_call(
        functools.partial(_gla_kernel, block=block),
        grid=(batch, nb),
        in_specs=[
            pl.BlockSpec((block, kd), lambda b, i: (rows(b, i), _COL_Q // kd)),
            pl.BlockSpec((block, kd), lambda b, i: (rows(b, i), _COL_K // kd)),
            pl.BlockSpec((block, vd), lambda b, i: (rows(b, i), _COL_V // vd)),
            pl.BlockSpec((block, vd), lambda b, i: (rows(b, i), _COL_G // vd)),
            pl.BlockSpec((block, LANES), lambda b, i: (rows(b, i), _COL_GKL // LANES)),
            _const_spec((LANES, kd)),
            _const_spec((1, kd)),
            _const_spec((1, GLA_HEAD_V)),
            _const_spec((GLA_CHUNK, GLA_CHUNK)),
        ],
        out_specs=pl.BlockSpec((block, vd), lambda b, i: (rows(b, i), 0)),
        out_shape=jax.ShapeDtypeStruct((t, vd), BF16),
        scratch_shapes=[pltpu.VMEM((GLA_HEADS, GLA_HEAD_V, GLA_HEAD_K), F32)],
        compiler_params=_cparams("parallel", "arbitrary"),
        name="gla_scan",
    )(proj, proj, proj, proj, proj, w2, gk_b.reshape(1, kd), norm_w.reshape(1, GLA_HEAD_V), tril)


_DQ_CQ = 0
_DQ_CKV = 512
_DQ_KRA = 768
_DQ_KRB = 896
_DQ_COLS = 1024
_QKV_COLS = MLA_HEADS * HEAD_PAD
V7X_MXU_COLS = 256
_HEAD_GROUP = V7X_MXU_COLS // HEAD_PAD
_GROUP_COLS = _HEAD_GROUP * HEAD_PAD


def _mla_qkv_kernel(x_ref, nw_ref, wd_ref, qlw_ref, wa_ref, wb_ref, ga_ref, gb_ref, klw_ref, wk_ref, wv_ref,
                    gk_ref, gra_ref, grb_ref, cos_ref, sin_ref, pm_ref, ones_ref, q_ref, k_ref, v_ref, c_ref,
                    *, scale):
    h = _rms(x_ref[...], nw_ref[...]).astype(BF16)
    for c0, c1 in _col_chunks(_DQ_COLS, 512):
        c_ref[:, c0:c1] = _dot(h, wd_ref[:, c0:c1])
    cos = cos_ref[...]
    sin = sin_ref[...]
    pm = pm_ref[...]

    cq = _rms(c_ref[:, _DQ_CQ:_DQ_CQ + MLA_Q_RANK], qlw_ref[...]).astype(BF16)
    fa = jnp.concatenate([ga_ref[...] * cos * scale] * _HEAD_GROUP, axis=1)
    fb = jnp.concatenate([gb_ref[...] * sin * scale] * _HEAD_GROUP, axis=1)
    for g in range(MLA_HEADS // _HEAD_GROUP):
        sl = slice(g * _GROUP_COLS, (g + 1) * _GROUP_COLS)
        a = _dot(cq, wa_ref[:, sl])
        b = _dot(cq, wb_ref[:, sl])
        ms = _dot((a * a).astype(BF16), pm)
        q_ref[:, sl] = ((a * fa + b * fb) * lax.rsqrt(ms + NORM_EPS)).astype(q_ref.dtype)

    ckv = _rms(c_ref[:, _DQ_CKV:_DQ_CKV + MLA_KV_RANK], klw_ref[...]).astype(BF16)
    kra = c_ref[:, _DQ_KRA:_DQ_KRA + HEAD_PAD]
    krb = c_ref[:, _DQ_KRB:_DQ_KRB + HEAD_PAD]
    r_rope = lax.rsqrt(jnp.sum(kra * kra, axis=-1, keepdims=True) * (1.0 / MLA_ROPE) + NORM_EPS)
    k_rope = (kra * (gra_ref[...] * cos) + krb * (grb_ref[...] * sin)) * r_rope
    k_rope = jnp.concatenate([k_rope] * _HEAD_GROUP, axis=1)
    gk = jnp.concatenate([gk_ref[...]] * _HEAD_GROUP, axis=1)
    ones = jnp.concatenate([ones_ref[...]] * _HEAD_GROUP, axis=1)
    for g in range(MLA_HEADS // _HEAD_GROUP):
        sl = slice(g * _GROUP_COLS, (g + 1) * _GROUP_COLS)
        kn = _dot(ckv, wk_ref[:, sl])
        ms = _dot((kn * kn).astype(BF16), pm)
        k_ref[:, sl] = (kn * gk * lax.rsqrt(ms + NORM_EPS) + k_rope).astype(k_ref.dtype)
        v_ref[:, sl] = (_dot(ckv, wv_ref[:, sl]) + ones).astype(v_ref.dtype)


def _part_mean_matrix():
    lane = jnp.arange(_GROUP_COLS)
    head, pos = lane // HEAD_PAD, lane % HEAD_PAD
    part = jnp.where(pos < MLA_NOPE, 0, jnp.where(pos < MLA_NOPE + MLA_ROPE, 1, 2))
    size = jnp.where(part == 0, MLA_NOPE, MLA_ROPE).astype(F32)
    same = (head[:, None] == head[None, :]) & (part[:, None] == part[None, :]) & (part[:, None] < 2)
    return (same.astype(F32) / size[None, :]).astype(BF16)


def _rope_tables(seqlen, batch):
    pos = jnp.arange(seqlen, dtype=F32)
    inv = 1.0 / (ROPE_THETA ** (jnp.arange(0, MLA_ROPE, 2, dtype=F32) / MLA_ROPE))
    ang = pos[:, None] * inv[None, :]
    cos, sin = jnp.cos(ang), jnp.sin(ang)
    pad = jnp.zeros((seqlen, HEAD_PAD - MLA_NOPE - MLA_ROPE), F32)
    cos_t = jnp.concatenate([jnp.ones((seqlen, MLA_NOPE), F32), cos, cos, pad], axis=1)
    sin_t = jnp.concatenate([jnp.zeros((seqlen, MLA_NOPE), F32), -sin, sin, pad], axis=1)
    return jnp.tile(cos_t, (batch, 1)), jnp.tile(sin_t, (batch, 1))


def _swap_halves(w):
    half = w.shape[-1] // 2
    return jnp.concatenate([w[..., half:], w[..., :half]], axis=-1)


def _rope_lanes(w):
    pad = [(0, 0)] * (w.ndim - 1) + [(MLA_NOPE, HEAD_PAD - MLA_NOPE - MLA_ROPE)]
    return jnp.pad(w, pad)


def mla_qkv(x, norm_w, w_dqkv, q_lora_norm, w_uq, kv_lora_norm, w_ukv, q_nope_norm, q_rope_norm,
            k_nope_norm, k_rope_norm, cos_t, sin_t, *, tm=512):
    t, d = x.shape
    pad_rope = HEAD_PAD - MLA_NOPE - MLA_ROPE
    w_cq = w_dqkv[:, :MLA_Q_RANK]
    w_ckv = w_dqkv[:, MLA_Q_RANK:MLA_Q_RANK + MLA_KV_RANK]
    w_kr = w_dqkv[:, MLA_Q_RANK + MLA_KV_RANK:]
    wd = jnp.concatenate(
        [w_cq, jnp.zeros((d, _DQ_CKV - MLA_Q_RANK), F32), w_ckv, _rope_lanes(w_kr),
         _rope_lanes(_swap_halves(w_kr))], axis=1).astype(BF16)
    wq = w_uq.reshape(MLA_Q_RANK, MLA_HEADS, MLA_NOPE + MLA_ROPE)
    wq_nope, wq_rope = wq[..., :MLA_NOPE], wq[..., MLA_NOPE:]
    wa = jnp.concatenate([wq_nope, wq_rope, jnp.zeros((MLA_Q_RANK, MLA_HEADS, pad_rope), F32)], axis=-1)
    wa = wa.reshape(MLA_Q_RANK, _QKV_COLS).astype(BF16)
    wb = _rope_lanes(_swap_halves(wq_rope)).reshape(MLA_Q_RANK, _QKV_COLS).astype(BF16)
    ga = jnp.concatenate([q_nope_norm, q_rope_norm, jnp.zeros((pad_rope,), F32)]).reshape(1, HEAD_PAD)
    gb = _rope_lanes(_swap_halves(q_rope_norm)).reshape(1, HEAD_PAD)
    wkv = w_ukv.reshape(MLA_KV_RANK, MLA_HEADS, MLA_NOPE + MLA_V)
    wk = jnp.pad(wkv[..., :MLA_NOPE], [(0, 0), (0, 0), (0, HEAD_PAD - MLA_NOPE)])
    wv = jnp.pad(wkv[..., MLA_NOPE:], [(0, 0), (0, 0), (0, HEAD_PAD - MLA_V)])
    wk = wk.reshape(MLA_KV_RANK, _QKV_COLS).astype(BF16)
    wv = wv.reshape(MLA_KV_RANK, _QKV_COLS).astype(BF16)
    gk = jnp.pad(k_nope_norm, (0, HEAD_PAD - MLA_NOPE)).reshape(1, HEAD_PAD)
    gra = _rope_lanes(k_rope_norm).reshape(1, HEAD_PAD)
    grb = _rope_lanes(_swap_halves(k_rope_norm)).reshape(1, HEAD_PAD)
    ones = (jnp.arange(HEAD_PAD) >= MLA_V).astype(F32).reshape(1, HEAD_PAD)
    pm = _part_mean_matrix()
    row = lambda width: pl.BlockSpec((tm, width), lambda i: (i, 0))
    vec = lambda n: _const_spec((1, n))
    return pl.pallas_call(
        functools.partial(_mla_qkv_kernel, scale=(MLA_NOPE + MLA_ROPE) ** -0.5 * math.log2(math.e)),
        grid=(t // tm,),
        in_specs=[
            row(d), vec(d), _const_spec(wd.shape),
            vec(MLA_Q_RANK), _const_spec(wa.shape), _const_spec(wb.shape), vec(HEAD_PAD), vec(HEAD_PAD),
            vec(MLA_KV_RANK), _const_spec(wk.shape), _const_spec(wv.shape), vec(HEAD_PAD), vec(HEAD_PAD), vec(HEAD_PAD),
            row(HEAD_PAD), row(HEAD_PAD), _const_spec(pm.shape), vec(HEAD_PAD),
        ],
        out_specs=[row(_QKV_COLS)] * 3,
        out_shape=[jax.ShapeDtypeStruct((t, _QKV_COLS), BF16)] * 3,
        scratch_shapes=[pltpu.VMEM((tm, _DQ_COLS), F32)],
        compiler_params=_cparams("parallel"),
        name="mla_qkv",
    )(x, norm_w.reshape(1, d), wd, q_lora_norm.reshape(1, MLA_Q_RANK), wa, wb, ga, gb,
      kv_lora_norm.reshape(1, MLA_KV_RANK), wk, wv, gk, gra, grb, cos_t, sin_t, pm, ones)


_ATTN_UNROLL = 4
_F32_EXP_BIAS = 127
_F32_MANT_BITS = 23
_F32_MIN_EXP = -126


def _attn_kernel(q_ref, k_ref, v_ref, o_ref, s_ref, p_ref, mx_ref, m_ref, alpha_ref, acc_ref, *, tq, tk):
    qi = pl.program_id(2)

    def rows(j):
        return pl.ds(pl.multiple_of(j * tk, tk), tk)

    def stage_a(slot, j, limit=None, r0=0):
        nr = tq - r0
        s = _dot_nt(q_ref[r0:, :], k_ref[rows(j), :])
        if limit is not None:
            rel = (lax.broadcasted_iota(jnp.int32, (nr, tk), 1) - lax.broadcasted_iota(jnp.int32, (nr, tk), 0))
            s = jnp.where(rel <= limit + r0, s, -jnp.inf)
        s_ref[slot, r0:, :] = s
        mx_ref[slot, r0:, :] = jnp.broadcast_to(jnp.ceil(jnp.max(s, axis=-1, keepdims=True)), (nr, LANES))

    def stage_b(slot, r0=0):
        m = m_ref[r0:, :]
        m_new = jnp.maximum(m, mx_ref[slot, r0:, :])
        for c in range(0, tk, LANES):
            pb = jnp.exp2(s_ref[slot, r0:, c:c + LANES] - m_new).astype(BF16)
            p_ref[slot, r0 // 2:, c:c + LANES] = pltpu.bitcast(pb, jnp.uint32)
        m_ref[r0:, :] = m_new
        shift = jnp.maximum(m - m_new, float(_F32_MIN_EXP)).astype(jnp.int32)
        alpha_ref[r0:, :] = pltpu.bitcast(jnp.left_shift(shift + _F32_EXP_BIAS, _F32_MANT_BITS), F32)

    def stage_c(slot, j, r0=0):
        p = pltpu.bitcast(p_ref[slot, r0 // 2:, :], BF16)
        acc_ref[r0:, :] = alpha_ref[r0:, :] * acc_ref[r0:, :] + _dot(p, v_ref[rows(j), :])

    def trip(ii, limit):
        j = 2 * ii
        stage_a(1, j + 1)
        stage_c(1, jnp.maximum(j - 1, 0))
        stage_b(0)
        stage_a(0, j + 2, limit=limit)
        stage_c(0, j)
        stage_b(1)

    p_ref[1] = jnp.zeros((tq // 2, tk), jnp.uint32)
    m_ref[...] = jnp.full((tq, LANES), -jnp.inf, F32)
    alpha_ref[...] = jnp.ones((tq, LANES), F32)
    acc_ref[...] = jnp.zeros((tq, HEAD_PAD), F32)
    stage_a(0, 0, limit=qi * tq)

    n_plain = jnp.maximum(qi - 1, 0)
    n_body = jnp.right_shift(n_plain, _ATTN_UNROLL.bit_length() - 1)

    @pl.loop(0, n_body)
    def _(ib):
        for u in range(_ATTN_UNROLL):
            trip(_ATTN_UNROLL * ib + u, None)

    for r in range(_ATTN_UNROLL - 1):
        @pl.when(n_plain - _ATTN_UNROLL * n_body > r)
        def _():
            trip(_ATTN_UNROLL * n_body + r, None)

    @pl.when(qi > 0)
    def _():
        trip(qi - 1, 0)

    jd = 2 * qi
    stage_a(1, jd + 1, limit=-tk, r0=tk)
    stage_c(1, jnp.maximum(jd - 1, 0))
    stage_b(0)
    stage_c(0, jd)
    stage_b(1, r0=tk)
    stage_c(1, jd + 1, r0=tk)
    acc = acc_ref[...]
    o_ref[...] = (acc / acc[:, MLA_V:MLA_V + 1]).astype(o_ref.dtype)


def flash_attention(q, k, v, *, batch, tk=512):
    t = q.shape[0]
    seqlen = t // batch
    tq = 2 * tk
    nq = seqlen // tq
    return pl.pallas_call(
        functools.partial(_attn_kernel, tq=tq, tk=tk),
        grid=(batch, MLA_HEADS, nq),
        in_specs=[
            pl.BlockSpec((tq, HEAD_PAD), lambda b, h, i: (b * nq + i, h)),
            pl.BlockSpec((seqlen, HEAD_PAD), lambda b, h, i: (b, h)),
            pl.BlockSpec((seqlen, HEAD_PAD), lambda b, h, i: (b, h)),
        ],
        out_specs=pl.BlockSpec((tq, HEAD_PAD), lambda b, h, i: (b * nq + i, h)),
        out_shape=jax.ShapeDtypeStruct((t, _QKV_COLS), BF16),
        scratch_shapes=[
            pltpu.VMEM((2, tq, tk), F32),
            pltpu.VMEM((2, tq // 2, tk), jnp.uint32),
            pltpu.VMEM((2, tq, LANES), F32),
            pltpu.VMEM((tq, LANES), F32),
            pltpu.VMEM((tq, LANES), F32),
            pltpu.VMEM((tq, HEAD_PAD), F32),
        ],
        compiler_params=_cparams("parallel", "parallel", "arbitrary"),
        name="mla_attention",
    )(q, k, v)


def _even_mixer(x, batch, norm_w, w_in, conv_w, conv_b, dt_bias, a_log, d_skip, ssd_norm,
                gk_w2, gk_b, gla_norm, w_out):
    sizes = (D_SSM, CONV_DIM, SSD_HEADS, GLA_KEY_DIM, GLA_KEY_DIM, GLA_VAL_DIM, GLA_VAL_DIM, GLA_GATE_RANK)
    offs = [0]
    for s in sizes:
        offs.append(offs[-1] + s)
    seg = [w_in[:, offs[i]:offs[i + 1]] for i in range(len(sizes))]
    w_z, w_xbc, w_dt, w_q, w_k, w_v, w_g, w_gkl = seg
    w_main = jnp.concatenate(
        [w_z, w_v, w_g, w_xbc, w_q, w_k, jnp.pad(w_gkl, ((0, 0), (0, LANES - GLA_GATE_RANK)))], axis=1).astype(BF16)
    w_dtp = jnp.pad(w_dt, ((0, 0), (0, LANES - SSD_HEADS))).astype(BF16)
    proj, dt_raw = norm_proj(x, norm_w, [w_main, w_dtp], [BF16, F32])
    y_ssd = ssd_scan(proj, dt_raw, conv_w, conv_b, dt_bias, a_log, d_skip, ssd_norm, batch=batch)
    y_gla = gla_scan(proj, gk_w2, gk_b, gla_norm, batch=batch)
    w_o = w_out.astype(BF16)
    return [y_ssd, y_gla], [w_o[:D_SSM], w_o[D_SSM:]]


def _odd_mixer(x, batch, cos_t, sin_t, norm_w, w_dqkv, q_lora_norm, w_uq, kv_lora_norm, w_ukv,
               q_nope_norm, q_rope_norm, k_nope_norm, k_rope_norm, w_o):
    q, k, v = mla_qkv(x, norm_w, w_dqkv, q_lora_norm, w_uq, kv_lora_norm, w_ukv, q_nope_norm, q_rope_norm,
                      k_nope_norm, k_rope_norm, cos_t, sin_t)
    o = flash_attention(q, k, v, batch=batch)
    return [o], [w_o.astype(BF16)]


def kernel(x, mix_norm_even, w_in_even, conv_w, conv_b, dt_bias, a_log, d_skip, ssd_norm, gla_gk_w2, gla_gk_b, gla_norm, w_out_even, mix_norm_odd, w_dqkv, q_lora_norm, w_uq, kv_lora_norm, w_ukv, q_nope_norm, q_rope_norm, k_nope_norm, k_rope_norm, w_o_mla, ffn_norm, w_gate, w_up, w_down):
    batch, seqlen, d = x.shape
    depth = ffn_norm.shape[0]
    cos_t, sin_t = _rope_tables(seqlen, batch)
    x = x.reshape(batch * seqlen, d)
    for i in range(depth):
        j = i // 2
        if i % 2 == 0:
            acts, w_outs = _even_mixer(x, batch, mix_norm_even[j], w_in_even[j], conv_w[j], conv_b[j], dt_bias[j],
                                       a_log[j], d_skip[j], ssd_norm[j], gla_gk_w2[j], gla_gk_b[j], gla_norm[j],
                                       w_out_even[j])
        else:
            acts, w_outs = _odd_mixer(x, batch, cos_t, sin_t, mix_norm_odd[j], w_dqkv[j], q_lora_norm[j], w_uq[j],
                                      kv_lora_norm[j], w_ukv[j], q_nope_norm[j], q_rope_norm[j], k_nope_norm[j],
                                      k_rope_norm[j], w_o_mla[j])
        x = mix_ffn_residual(x, acts, w_outs, ffn_norm[i], w_gate[i].astype(BF16), w_up[i].astype(BF16),
                             w_down[i].astype(BF16))
    return x.reshape(batch, seqlen, d)
```

```python
import functools
import math

import jax
import jax.numpy as jnp
from jax import lax
from jax.experimental import pallas as pl
from jax.experimental.pallas import tpu as pltpu

F32 = jnp.float32
BF16 = jnp.bfloat16

D_MODEL = 1024
NORM_EPS = 1e-6

SSD_HEAD_DIM = 64
SSD_HEADS = 16
SSD_GROUPS = 2
SSD_HPG = SSD_HEADS // SSD_GROUPS
SSD_STATE = 128
SSD_CHUNK = 128
CONV_WIDTH = 4
D_SSM = SSD_HEADS * SSD_HEAD_DIM
CONV_DIM = D_SSM + 2 * SSD_GROUPS * SSD_STATE

GLA_HEADS = 4
GLA_KEY_DIM = D_MODEL // 2
GLA_VAL_DIM = D_MODEL
GLA_HEAD_K = GLA_KEY_DIM // GLA_HEADS
GLA_HEAD_V = GLA_VAL_DIM // GLA_HEADS
GLA_GATE_RANK = 16
GLA_GATE_NORM = 16.0
GLA_CHUNK = 64

MLA_HEADS = 16
MLA_NOPE = 64
MLA_ROPE = 32
MLA_V = 64
MLA_Q_RANK = 384
MLA_KV_RANK = 256
ROPE_THETA = 10000.0

LANES = 128
HEAD_PAD = 128
V7X_VMEM_BYTES = 64 * 1024 * 1024
VMEM_LIMIT = V7X_VMEM_BYTES - 8 * 1024 * 1024


def _cparams(*sem):
    return pltpu.CompilerParams(dimension_semantics=sem, vmem_limit_bytes=VMEM_LIMIT)


def _dot(a, b):
    return jnp.dot(a, b, preferred_element_type=F32)


def _dot_nt(a, b):
    return lax.dot_general(a, b, (((1,), (1,)), ((), ())), preferred_element_type=F32)


def _dot_tn(a, b):
    return lax.dot_general(a, b, (((0,), (0,)), ((), ())), preferred_element_type=F32)


def _split3(x):
    hi = x.astype(BF16)
    r1 = x - hi.astype(F32)
    mid = r1.astype(BF16)
    lo = (r1 - mid.astype(F32)).astype(BF16)
    return hi, mid, lo


def _dot_sel(x, m):
    hi, mid, lo = _split3(x)
    return _dot(hi, m) + _dot(mid, m) + _dot(lo, m)


def _sel_dot(m, x):
    hi, mid, lo = _split3(x)
    return _dot(m, hi) + _dot(m, mid) + _dot(m, lo)


def _rms(x, w):
    return x * lax.rsqrt(jnp.mean(x * x, axis=-1, keepdims=True) + NORM_EPS) * w


def _silu(x):
    return x * jax.nn.sigmoid(x)


def _softplus(x):
    return jnp.maximum(x, 0.0) + jnp.log1p(jnp.exp(-jnp.abs(x)))


def _col_chunks(n, step):
    return [(c, min(c + step, n)) for c in range(0, n, step)]


def _const_spec(shape):
    return pl.BlockSpec(shape, lambda *_: (0,) * len(shape), pipeline_mode=pl.Buffered(1))


def _norm_proj_kernel(*refs, n_out, chunk):
    x_ref, nw_ref = refs[0], refs[1]
    w_refs = refs[2:2 + n_out]
    o_refs = refs[2 + n_out:]
    h = _rms(x_ref[...].astype(F32), nw_ref[...]).astype(BF16)
    for w_ref, o_ref in zip(w_refs, o_refs):
        for c0, c1 in _col_chunks(w_ref.shape[1], chunk):
            o_ref[:, c0:c1] = _dot(h, w_ref[:, c0:c1]).astype(o_ref.dtype)


def norm_proj(x, norm_w, weights, out_dtypes, *, tm=512, chunk=512):
    t, k = x.shape
    n_out = len(weights)
    in_specs = [pl.BlockSpec((tm, k), lambda i: (i, 0)), _const_spec((1, k))]
    in_specs += [_const_spec(w.shape) for w in weights]
    out_specs = [pl.BlockSpec((tm, w.shape[1]), lambda i: (i, 0)) for w in weights]
    out_shape = [jax.ShapeDtypeStruct((t, w.shape[1]), dt) for w, dt in zip(weights, out_dtypes)]
    return pl.pallas_call(
        functools.partial(_norm_proj_kernel, n_out=n_out, chunk=chunk),
        grid=(t // tm,),
        in_specs=in_specs,
        out_specs=out_specs,
        out_shape=out_shape,
        compiler_params=_cparams("parallel"),
        name="norm_proj",
    )(x, norm_w.reshape(1, k), *weights)


def _drop_group_padding(a, keep):
    return jnp.concatenate([a[:, (g + 1) * LANES - keep:(g + 1) * LANES] for g in range(a.shape[1] // LANES)], axis=1)


def _mix_ffn_kernel(*refs, n_in, chunk):
    x_ref = refs[0]
    a_refs = refs[1:1 + n_in]
    w_refs = refs[1 + n_in:1 + 2 * n_in]
    nw_ref, wg_ref, wu_ref, wd_ref, o_ref = refs[1 + 2 * n_in:]
    x = x_ref[...]
    for a_ref, w_ref in zip(a_refs, w_refs):
        a = a_ref[...]
        if a.shape[1] != w_ref.shape[0]:
            a = _drop_group_padding(a, w_ref.shape[0] * LANES // a.shape[1])
        x = x + _dot(a, w_ref[...])
    o_ref[...] = x
    x = o_ref[...]
    h = _rms(x, nw_ref[...]).astype(BF16)
    acc = x
    for c0, c1 in _col_chunks(wg_ref.shape[1], chunk):
        g = _dot(h, wg_ref[:, c0:c1])
        u = _dot(h, wu_ref[:, c0:c1])
        a = (_silu(g) * u).astype(BF16)
        acc = acc + _dot(a, wd_ref[c0:c1, :])
    o_ref[...] = acc


def mix_ffn_residual(x, acts, weights, norm_w, wg, wu, wd, *, tm=512, chunk=256):
    t, d = x.shape
    n_in = len(acts)
    row = pl.BlockSpec((tm, d), lambda i: (i, 0))
    in_specs = [row]
    in_specs += [pl.BlockSpec((tm, a.shape[1]), lambda i: (i, 0)) for a in acts]
    in_specs += [_const_spec(w.shape) for w in weights]
    in_specs += [_const_spec((1, d)), _const_spec(wg.shape), _const_spec(wu.shape), _const_spec(wd.shape)]
    return pl.pallas_call(
        functools.partial(_mix_ffn_kernel, n_in=n_in, chunk=chunk),
        grid=(t // tm,),
        in_specs=in_specs,
        out_specs=row,
        out_shape=jax.ShapeDtypeStruct((t, d), F32),
        compiler_params=_cparams("parallel"),
        name="mix_ffn_residual",
    )(x, *acts, *weights, norm_w.reshape(1, d), wg, wu, wd)


_COL_Z = 0
_COL_V = _COL_Z + D_SSM
_COL_G = _COL_V + GLA_VAL_DIM
_COL_XBC = _COL_G + GLA_VAL_DIM
_COL_Q = _COL_XBC + CONV_DIM
_COL_K = _COL_Q + GLA_KEY_DIM
_COL_GKL = _COL_K + GLA_KEY_DIM
_PROJ_COLS = _COL_GKL + LANES
_CONV_HALO = 8
_SSD_STEP_CHUNKS = 1


def _ssd_kernel(z_ref, xs_ref, bc_ref, dt_ref, cw_ref, cb_ref, dtb_ref, nega_ref, dskip_ref, nw_ref,
                tril_ref, exp_ref, rep_ref, o_ref, xpad_ref, state_ref):
    q = SSD_CHUNK
    p = SSD_HEAD_DIM
    n = SSD_STATE
    gw = SSD_HPG * p

    @pl.when(pl.program_id(1) == 0)
    def _():
        xpad_ref[0:_CONV_HALO, :] = jnp.zeros((_CONV_HALO, CONV_DIM), F32)
        state_ref[...] = jnp.zeros_like(state_ref)

    row = lax.broadcasted_iota(jnp.int32, (q, q), 0)
    col = lax.broadcasted_iota(jnp.int32, (q, q), 1)
    causal = row >= col
    lane = lax.broadcasted_iota(jnp.int32, (q, 2 * p), 1)
    for sub in range(z_ref.shape[0] // q):
        rs = slice(sub * q, (sub + 1) * q)
        xpad_ref[_CONV_HALO:, 0:D_SSM] = xs_ref[rs, :].astype(F32)
        xpad_ref[_CONV_HALO:, D_SSM:] = bc_ref[rs, :].astype(F32)
        conv = cb_ref[...]
        for j in range(CONV_WIDTH):
            off = _CONV_HALO - (CONV_WIDTH - 1) + j
            conv = conv + cw_ref[j:j + 1, :] * xpad_ref[off:off + q, :]
        xpad_ref[0:_CONV_HALO, :] = xpad_ref[q:q + _CONV_HALO, :]
        xbc = _silu(conv)

        dt = _softplus(dt_ref[rs, :] + dtb_ref[...])
        a = dt * nega_ref[...]
        acum = _sel_dot(tril_ref[...], a)
        acum_t = acum.T
        dt_x = _dot_sel(dt, exp_ref[...])
        acum_x = _dot_sel(acum, exp_ref[...])
        acum_rep = _dot_sel(acum, rep_ref[...])
        a_last_x = acum_x[q - 1:q, :]
        xs = xbc[:, 0:D_SSM]
        xdt = xs * dt_x
        xdt_bf = xdt.astype(BF16)
        w_end = (xdt * jnp.exp(a_last_x - acum_x)).astype(BF16)
        exp_a = jnp.exp(acum_x)
        chunk_decay = jnp.exp(a_last_x)
        y_skip = xs * dskip_ref[...]

        y_groups = []
        for g in range(SSD_GROUPS):
            gs = slice(g * gw, (g + 1) * gw)
            b_g = xbc[:, D_SSM + g * n:D_SSM + (g + 1) * n]
            c_g = xbc[:, D_SSM + (SSD_GROUPS + g) * n:D_SSM + (SSD_GROUPS + g + 1) * n]
            b_bf = b_g.astype(BF16)
            c_bf = c_g.astype(BF16)
            cb = _dot_nt(c_bf, b_bf)
            s_g = state_ref[g]
            y_off = _dot(c_bf, s_g.astype(BF16)) * exp_a[:, gs]
            y_pairs = []
            for k in range(0, SSD_HPG, 2):
                lhs = []
                for hk in (g * SSD_HPG + k, g * SSD_HPG + k + 1):
                    a_col = acum_rep[:, hk * q:(hk + 1) * q]
                    a_row = acum_t[hk:hk + 1, :]
                    decay = jnp.where(causal, jnp.exp(jnp.minimum(a_col - a_row, 0.0)), 0.0)
                    lhs.append((cb * decay).astype(BF16))
                hk0 = g * SSD_HPG + k
                x_pair = xdt_bf[:, hk0 * p:(hk0 + 2) * p]
                zero = jnp.zeros_like(x_pair)
                rhs = jnp.concatenate([jnp.where(lane < p, x_pair, zero), jnp.where(lane >= p, x_pair, zero)], axis=0)
                y_pairs.append(_dot(jnp.concatenate(lhs, axis=1), rhs))
            state_ref[g] = s_g * chunk_decay[:, gs] + _dot_tn(b_bf, w_end[:, gs])
            y_g = jnp.concatenate(y_pairs, axis=1) + y_off + y_skip[:, gs]
            y_g = y_g * _silu(z_ref[rs, gs].astype(F32))
            y_groups.append(_rms(y_g, nw_ref[:, gs]))
        o_ref[rs, :] = jnp.concatenate(y_groups, axis=1).astype(o_ref.dtype)


def ssd_scan(proj, dt_raw, conv_w, conv_b, dt_bias, a_log, d_skip, norm_w, *, batch):
    t = proj.shape[0]
    q = SSD_CHUNK
    br = _SSD_STEP_CHUNKS * q
    nc = t // batch // br

    def pad_heads(v):
        return jnp.pad(v.reshape(1, SSD_HEADS), ((0, 0), (0, LANES - SSD_HEADS)))

    tril = jnp.tril(jnp.ones((q, q), F32)).astype(BF16)
    head = jnp.arange(LANES)[:, None]
    expand = (head == jnp.arange(D_SSM)[None, :] // SSD_HEAD_DIM).astype(BF16)
    replicate = (head == jnp.arange(SSD_HEADS * q)[None, :] // q).astype(BF16)
    rows = lambda b, c: b * nc + c
    return pl.pallas_call(
        _ssd_kernel,
        grid=(batch, nc),
        in_specs=[
            pl.BlockSpec((br, D_SSM), lambda b, c: (rows(b, c), _COL_Z // D_SSM)),
            pl.BlockSpec((br, D_SSM), lambda b, c: (rows(b, c), _COL_XBC // D_SSM)),
            pl.BlockSpec((br, CONV_DIM - D_SSM), lambda b, c: (rows(b, c), (_COL_XBC + D_SSM) // (CONV_DIM - D_SSM))),
            pl.BlockSpec((br, LANES), lambda b, c: (rows(b, c), 0)),
            _const_spec((CONV_WIDTH, CONV_DIM)),
            _const_spec((1, CONV_DIM)),
            _const_spec((1, LANES)),
            _const_spec((1, LANES)),
            _const_spec((1, D_SSM)),
            _const_spec((1, D_SSM)),
            _const_spec((q, q)),
            _const_spec(expand.shape),
            _const_spec(replicate.shape),
        ],
        out_specs=pl.BlockSpec((br, D_SSM), lambda b, c: (rows(b, c), 0)),
        out_shape=jax.ShapeDtypeStruct((t, D_SSM), BF16),
        scratch_shapes=[
            pltpu.VMEM((q + _CONV_HALO, CONV_DIM), F32),
            pltpu.VMEM((SSD_GROUPS, SSD_STATE, SSD_HPG * SSD_HEAD_DIM), F32),
        ],
        compiler_params=_cparams("parallel", "arbitrary"),
        name="ssd_scan",
    )(proj, proj, proj, dt_raw, conv_w, conv_b.reshape(1, CONV_DIM), pad_heads(dt_bias),
      pad_heads(-jnp.exp(a_log)), jnp.repeat(d_skip, SSD_HEAD_DIM).reshape(1, D_SSM), norm_w.reshape(1, D_SSM),
      tril, expand, replicate)


def _gla_kernel(q_ref, k_ref, v_ref, g_ref, gkl_ref, w2_ref, gkb_ref, nw_ref, tril_ref, o_ref, state_ref,
                *, block):
    c = GLA_CHUNK
    dk, dv = GLA_HEAD_K, GLA_HEAD_V
    scale = dk ** -0.5

    @pl.when(pl.program_id(1) == 0)
    def _():
        state_ref[...] = jnp.zeros_like(state_ref)

    gk = -_softplus(-(_dot(gkl_ref[...], w2_ref[...]) + gkb_ref[...])) * (1.0 / GLA_GATE_NORM)
    row = lax.broadcasted_iota(jnp.int32, (c, c), 0)
    col = lax.broadcasted_iota(jnp.int32, (c, c), 1)
    causal = row >= col
    for j in range(block // c):
        r0, r1 = j * c, (j + 1) * c
        gcum_all = _sel_dot(tril_ref[...], gk[r0:r1])
        for h in range(GLA_HEADS):
            ks, vs = slice(h * dk, (h + 1) * dk), slice(h * dv, (h + 1) * dv)
            gcum = gcum_all[:, ks]
            g_last = gcum[c - 1:c, :]
            qc = q_ref[r0:r1, ks].astype(F32) * scale
            kc = k_ref[r0:r1, ks].astype(F32)
            vc = v_ref[r0:r1, vs]
            q_dec = (qc * jnp.exp(gcum)).astype(BF16)
            k_inv = (kc * jnp.exp(-gcum)).astype(BF16)
            k_end = (kc * jnp.exp(g_last - gcum)).astype(BF16)
            scores = jnp.where(causal, _dot_nt(q_dec, k_inv), 0.0)
            s_t = state_ref[h]
            o = _dot(scores.astype(BF16), vc) + _dot_nt(q_dec, s_t.astype(BF16))
            state_ref[h] = s_t * jnp.exp(g_last) + _dot_tn(vc, k_end)
            o = _rms(o, nw_ref[...]) * _silu(g_ref[r0:r1, vs].astype(F32))
            o_ref[r0:r1, vs] = o.astype(o_ref.dtype)


def gla_scan(proj, gk_w2, gk_b, norm_w, *, batch, block=512):
    t = proj.shape[0]
    nb = t // batch // block
    kd, vd = GLA_KEY_DIM, GLA_VAL_DIM
    w2 = jnp.pad(gk_w2, ((0, LANES - GLA_GATE_RANK), (0, 0))).astype(BF16)
    tril = jnp.tril(jnp.ones((GLA_CHUNK, GLA_CHUNK), F32)).astype(BF16)
    rows = lambda b, i: b * nb + i
    return pl.pallas_call(
        functools.partial(_gla_kernel, block=block),
        grid=(batch, nb),
        in_specs=[
            pl.BlockSpec((block, kd), lambda b, i: (rows(b, i), _COL_Q // kd)),
            pl.BlockSpec((block, kd), lambda b, i: (rows(b, i), _COL_K // kd)),
            pl.BlockSpec((block, vd), lambda b, i: (rows(b, i), _COL_V // vd)),
            pl.BlockSpec((block, vd), lambda b, i: (rows(b, i), _COL_G // vd)),
            pl.BlockSpec((block, LANES), lambda b, i: (rows(b, i), _COL_GKL // LANES)),
            _const_spec((LANES, kd)),
            _const_spec((1, kd)),
            _const_spec((1, GLA_HEAD_V)),
            _const_spec((GLA_CHUNK, GLA_CHUNK)),
        ],
        out_specs=pl.BlockSpec((block, vd), lambda b, i: (rows(b, i), 0)),
        out_shape=jax.ShapeDtypeStruct((t, vd), BF16),
        scratch_shapes=[pltpu.VMEM((GLA_HEADS, GLA_HEAD_V, GLA_HEAD_K), F32)],
        compiler_params=_cparams("parallel", "arbitrary"),
        name="gla_scan",
    )(proj, proj, proj, proj, proj, w2, gk_b.reshape(1, kd), norm_w.reshape(1, GLA_HEAD_V), tril)


_DQ_CQ = 0
_DQ_CKV = 512
_DQ_KRA = 768
_DQ_KRB = 896
_DQ_COLS = 1024
_QKV_COLS = MLA_HEADS * HEAD_PAD
V7X_MXU_COLS = 256
_HEAD_GROUP = V7X_MXU_COLS // HEAD_PAD
_GROUP_COLS = _HEAD_GROUP * HEAD_PAD


def _mla_qkv_kernel(x_ref, nw_ref, wd_ref, qlw_ref, wa_ref, wb_ref, ga_ref, gb_ref, klw_ref, wkv_ref,
                    gk_ref, gra_ref, grb_ref, cos_ref, sin_ref, pm_ref, ones_ref, q_ref, k_ref, v_ref, c_ref,
                    *, scale):
    h = _rms(x_ref[...], nw_ref[...]).astype(BF16)
    for c0, c1 in _col_chunks(_DQ_COLS, 512):
        c_ref[:, c0:c1] = _dot(h, wd_ref[:, c0:c1])
    cos = cos_ref[...]
    sin = sin_ref[...]
    pm = pm_ref[...]

    cq = _rms(c_ref[:, _DQ_CQ:_DQ_CQ + MLA_Q_RANK], qlw_ref[...]).astype(BF16)
    fa = jnp.concatenate([ga_ref[...] * cos * scale] * _HEAD_GROUP, axis=1)
    fb = jnp.concatenate([gb_ref[...] * sin * scale] * _HEAD_GROUP, axis=1)
    for g in range(MLA_HEADS // _HEAD_GROUP):
        sl = slice(g * _GROUP_COLS, (g + 1) * _GROUP_COLS)
        a = _dot(cq, wa_ref[:, sl])
        b = _dot(cq, wb_ref[:, sl])
        ms = _dot((a * a).astype(BF16), pm)
        q_ref[:, sl] = ((a * fa + b * fb) * lax.rsqrt(ms + NORM_EPS)).astype(q_ref.dtype)

    ckv = _rms(c_ref[:, _DQ_CKV:_DQ_CKV + MLA_KV_RANK], klw_ref[...]).astype(BF16)
    kra = c_ref[:, _DQ_KRA:_DQ_KRA + HEAD_PAD]
    krb = c_ref[:, _DQ_KRB:_DQ_KRB + HEAD_PAD]
    r_rope = lax.rsqrt(jnp.sum(kra * kra, axis=-1, keepdims=True) * (1.0 / MLA_ROPE) + NORM_EPS)
    k_rope = (kra * (gra_ref[...] * cos) + krb * (grb_ref[...] * sin)) * r_rope
    k_rope = jnp.concatenate([k_rope] * _HEAD_GROUP, axis=1)
    gk = jnp.concatenate([gk_ref[...]] * _HEAD_GROUP, axis=1)
    ones = jnp.concatenate([ones_ref[...]] * _HEAD_GROUP, axis=1)
    keep_v = 1.0 - ones
    for g in range(MLA_HEADS // _HEAD_GROUP):
        sl = slice(g * _GROUP_COLS, (g + 1) * _GROUP_COLS)
        kv = _dot(ckv, wkv_ref[:, sl])
        ms = _dot((kv * kv).astype(BF16), pm)
        k_ref[:, sl] = (kv * gk * lax.rsqrt(ms + NORM_EPS) + k_rope).astype(k_ref.dtype)
        v_ref[:, sl] = (kv * keep_v + ones).astype(v_ref.dtype)


def _part_mean_matrix():
    lane = jnp.arange(_GROUP_COLS)
    head, pos = lane // HEAD_PAD, lane % HEAD_PAD
    part = jnp.where(pos < MLA_NOPE, 0, jnp.where(pos < MLA_NOPE + MLA_ROPE, 1, 2))
    size = jnp.where(part == 0, MLA_NOPE, MLA_ROPE).astype(F32)
    same = (head[:, None] == head[None, :]) & (part[:, None] == part[None, :]) & (part[:, None] < 2)
    return (same.astype(F32) / size[None, :]).astype(BF16)


def _rope_tables(seqlen, batch):
    pos = jnp.arange(seqlen, dtype=F32)
    inv = 1.0 / (ROPE_THETA ** (jnp.arange(0, MLA_ROPE, 2, dtype=F32) / MLA_ROPE))
    ang = pos[:, None] * inv[None, :]
    cos, sin = jnp.cos(ang), jnp.sin(ang)
    pad = jnp.zeros((seqlen, HEAD_PAD - MLA_NOPE - MLA_ROPE), F32)
    cos_t = jnp.concatenate([jnp.ones((seqlen, MLA_NOPE), F32), cos, cos, pad], axis=1)
    sin_t = jnp.concatenate([jnp.zeros((seqlen, MLA_NOPE), F32), -sin, sin, pad], axis=1)
    return jnp.tile(cos_t, (batch, 1)), jnp.tile(sin_t, (batch, 1))


def _swap_halves(w):
    half = w.shape[-1] // 2
    return jnp.concatenate([w[..., half:], w[..., :half]], axis=-1)


def _rope_lanes(w):
    pad = [(0, 0)] * (w.ndim - 1) + [(MLA_NOPE, HEAD_PAD - MLA_NOPE - MLA_ROPE)]
    return jnp.pad(w, pad)


def mla_qkv(x, norm_w, w_dqkv, q_lora_norm, w_uq, kv_lora_norm, w_ukv, q_nope_norm, q_rope_norm,
            k_nope_norm, k_rope_norm, cos_t, sin_t, *, tm=512):
    t, d = x.shape
    pad_rope = HEAD_PAD - MLA_NOPE - MLA_ROPE
    w_cq = w_dqkv[:, :MLA_Q_RANK]
    w_ckv = w_dqkv[:, MLA_Q_RANK:MLA_Q_RANK + MLA_KV_RANK]
    w_kr = w_dqkv[:, MLA_Q_RANK + MLA_KV_RANK:]
    wd = jnp.concatenate(
        [w_cq, jnp.zeros((d, _DQ_CKV - MLA_Q_RANK), F32), w_ckv, _rope_lanes(w_kr),
         _rope_lanes(_swap_halves(w_kr))], axis=1).astype(BF16)
    wq = w_uq.reshape(MLA_Q_RANK, MLA_HEADS, MLA_NOPE + MLA_ROPE)
    wq_nope, wq_rope = wq[..., :MLA_NOPE], wq[..., MLA_NOPE:]
    wa = jnp.concatenate([wq_nope, wq_rope, jnp.zeros((MLA_Q_RANK, MLA_HEADS, pad_rope), F32)], axis=-1)
    wa = wa.reshape(MLA_Q_RANK, _QKV_COLS).astype(BF16)
    wb = _rope_lanes(_swap_halves(wq_rope)).reshape(MLA_Q_RANK, _QKV_COLS).astype(BF16)
    ga = jnp.concatenate([q_nope_norm, q_rope_norm, jnp.zeros((pad_rope,), F32)]).reshape(1, HEAD_PAD)
    gb = _rope_lanes(_swap_halves(q_rope_norm)).reshape(1, HEAD_PAD)
    wkv = w_ukv.reshape(MLA_KV_RANK, MLA_HEADS, MLA_NOPE + MLA_V)
    assert MLA_NOPE + MLA_V == HEAD_PAD
    wkv = wkv.reshape(MLA_KV_RANK, _QKV_COLS).astype(BF16)
    gk = jnp.pad(k_nope_norm, (0, HEAD_PAD - MLA_NOPE)).reshape(1, HEAD_PAD)
    gra = _rope_lanes(k_rope_norm).reshape(1, HEAD_PAD)
    grb = _rope_lanes(_swap_halves(k_rope_norm)).reshape(1, HEAD_PAD)
    ones = (jnp.arange(HEAD_PAD) < MLA_NOPE).astype(F32).reshape(1, HEAD_PAD)
    pm = _part_mean_matrix()
    row = lambda width: pl.BlockSpec((tm, width), lambda i: (i, 0))
    vec = lambda n: _const_spec((1, n))
    return pl.pallas_call(
        functools.partial(_mla_qkv_kernel, scale=(MLA_NOPE + MLA_ROPE) ** -0.5 * math.log2(math.e)),
        grid=(t // tm,),
        in_specs=[
            row(d), vec(d), _const_spec(wd.shape),
            vec(MLA_Q_RANK), _const_spec(wa.shape), _const_spec(wb.shape), vec(HEAD_PAD), vec(HEAD_PAD),
            vec(MLA_KV_RANK), _const_spec(wkv.shape), vec(HEAD_PAD), vec(HEAD_PAD), vec(HEAD_PAD),
            row(HEAD_PAD), row(HEAD_PAD), _const_spec(pm.shape), vec(HEAD_PAD),
        ],
        out_specs=[row(_QKV_COLS)] * 3,
        out_shape=[jax.ShapeDtypeStruct((t, _QKV_COLS), BF16)] * 3,
        scratch_shapes=[pltpu.VMEM((tm, _DQ_COLS), F32)],
        compiler_params=_cparams("parallel"),
        name="mla_qkv",
    )(x, norm_w.reshape(1, d), wd, q_lora_norm.reshape(1, MLA_Q_RANK), wa, wb, ga, gb,
      kv_lora_norm.reshape(1, MLA_KV_RANK), wkv, gk, gra, grb, cos_t, sin_t, pm, ones)


_ATTN_UNROLL = 4
_F32_EXP_BIAS = 127
_F32_MANT_BITS = 23
_F32_MIN_EXP = -126


def _attn_kernel(q_ref, k_ref, v_ref, o_ref, s_ref, p_ref, mx_ref, m_ref, alpha_ref, acc_ref, *, tq, tk):
    qi = pl.program_id(2)

    def rows(j):
        return pl.ds(pl.multiple_of(j * tk, tk), tk)

    def stage_a(slot, j, limit=None, r0=0):
        nr = tq - r0
        s = _dot_nt(q_ref[r0:, :], k_ref[rows(j), :])
        if limit is not None:
            rel = (lax.broadcasted_iota(jnp.int32, (nr, tk), 1) - lax.broadcasted_iota(jnp.int32, (nr, tk), 0))
            s = jnp.where(rel <= limit + r0, s, -jnp.inf)
        s_ref[slot, r0:, :] = s
        mx_ref[slot, r0:, :] = jnp.broadcast_to(jnp.ceil(jnp.max(s, axis=-1, keepdims=True)), (nr, LANES))

    def stage_b(slot, r0=0):
        m = m_ref[r0:, :]
        m_new = jnp.maximum(m, mx_ref[slot, r0:, :])
        for c in range(0, tk, LANES):
            pb = jnp.exp2(s_ref[slot, r0:, c:c + LANES] - m_new).astype(BF16)
            p_ref[slot, r0 // 2:, c:c + LANES] = pltpu.bitcast(pb, jnp.uint32)
        m_ref[r0:, :] = m_new
        shift = jnp.maximum(m - m_new, float(_F32_MIN_EXP)).astype(jnp.int32)
        alpha_ref[r0:, :] = pltpu.bitcast(jnp.left_shift(shift + _F32_EXP_BIAS, _F32_MANT_BITS), F32)

    def stage_c(slot, j, r0=0):
        p = pltpu.bitcast(p_ref[slot, r0 // 2:, :], BF16)
        acc_ref[r0:, :] = alpha_ref[r0:, :] * acc_ref[r0:, :] + _dot(p, v_ref[rows(j), :])

    def trip(ii, limit):
        j = 2 * ii
        stage_a(1, j + 1)
        stage_c(1, jnp.maximum(j - 1, 0))
        stage_b(0)
        stage_a(0, j + 2, limit=limit)
        stage_c(0, j)
        stage_b(1)

    p_ref[1] = jnp.zeros((tq // 2, tk), jnp.uint32)
    m_ref[...] = jnp.full((tq, LANES), -jnp.inf, F32)
    alpha_ref[...] = jnp.ones((tq, LANES), F32)
    acc_ref[...] = jnp.zeros((tq, HEAD_PAD), F32)
    stage_a(0, 0, limit=qi * tq)

    n_plain = jnp.maximum(qi - 1, 0)
    n_body = jnp.right_shift(n_plain, _ATTN_UNROLL.bit_length() - 1)

    @pl.loop(0, n_body)
    def _(ib):
        for u in range(_ATTN_UNROLL):
            trip(_ATTN_UNROLL * ib + u, None)

    for r in range(_ATTN_UNROLL - 1):
        @pl.when(n_plain - _ATTN_UNROLL * n_body > r)
        def _():
            trip(_ATTN_UNROLL * n_body + r, None)

    @pl.when(qi > 0)
    def _():
        trip(qi - 1, 0)

    jd = 2 * qi
    stage_a(1, jd + 1, limit=-tk, r0=tk)
    stage_c(1, jnp.maximum(jd - 1, 0))
    stage_b(0)
    stage_c(0, jd)
    stage_b(1, r0=tk)
    stage_c(1, jd + 1, r0=tk)
    acc = acc_ref[...]
    o_ref[...] = (acc / acc[:, 0:1]).astype(o_ref.dtype)


def flash_attention(q, k, v, *, batch, tk=512):
    t = q.shape[0]
    seqlen = t // batch
    tq = 2 * tk
    nq = seqlen // tq
    return pl.pallas_call(
        functools.partial(_attn_kernel, tq=tq, tk=tk),
        grid=(batch, MLA_HEADS, nq),
        in_specs=[
            pl.BlockSpec((tq, HEAD_PAD), lambda b, h, i: (b * nq + i, h)),
            pl.BlockSpec((seqlen, HEAD_PAD), lambda b, h, i: (b, h)),
            pl.BlockSpec((seqlen, HEAD_PAD), lambda b, h, i: (b, h)),
        ],
        out_specs=pl.BlockSpec((tq, HEAD_PAD), lambda b, h, i: (b * nq + i, h)),
        out_shape=jax.ShapeDtypeStruct((t, _QKV_COLS), BF16),
        scratch_shapes=[
            pltpu.VMEM((2, tq, tk), F32),
            pltpu.VMEM((2, tq // 2, tk), jnp.uint32),
            pltpu.VMEM((2, tq, LANES), F32),
            pltpu.VMEM((tq, LANES), F32),
            pltpu.VMEM((tq, LANES), F32),
            pltpu.VMEM((tq, HEAD_PAD), F32),
        ],
        compiler_params=_cparams("parallel", "parallel", "arbitrary"),
        name="mla_attention",
    )(q, k, v)


def _even_mixer(x, batch, norm_w, w_in, conv_w, conv_b, dt_bias, a_log, d_skip, ssd_norm,
                gk_w2, gk_b, gla_norm, w_out):
    sizes = (D_SSM, CONV_DIM, SSD_HEADS, GLA_KEY_DIM, GLA_KEY_DIM, GLA_VAL_DIM, GLA_VAL_DIM, GLA_GATE_RANK)
    offs = [0]
    for s in sizes:
        offs.append(offs[-1] + s)
    seg = [w_in[:, offs[i]:offs[i + 1]] for i in range(len(sizes))]
    w_z, w_xbc, w_dt, w_q, w_k, w_v, w_g, w_gkl = seg
    w_main = jnp.concatenate(
        [w_z, w_v, w_g, w_xbc, w_q, w_k, jnp.pad(w_gkl, ((0, 0), (0, LANES - GLA_GATE_RANK)))], axis=1).astype(BF16)
    w_dtp = jnp.pad(w_dt, ((0, 0), (0, LANES - SSD_HEADS))).astype(BF16)
    proj, dt_raw = norm_proj(x, norm_w, [w_main, w_dtp], [BF16, F32])
    y_ssd = ssd_scan(proj, dt_raw, conv_w, conv_b, dt_bias, a_log, d_skip, ssd_norm, batch=batch)
    y_gla = gla_scan(proj, gk_w2, gk_b, gla_norm, batch=batch)
    w_o = w_out.astype(BF16)
    return [y_ssd, y_gla], [w_o[:D_SSM], w_o[D_SSM:]]


def _odd_mixer(x, batch, cos_t, sin_t, norm_w, w_dqkv, q_lora_norm, w_uq, kv_lora_norm, w_ukv,
               q_nope_norm, q_rope_norm, k_nope_norm, k_rope_norm, w_o):
    q, k, v = mla_qkv(x, norm_w, w_dqkv, q_lora_norm, w_uq, kv_lora_norm, w_ukv, q_nope_norm, q_rope_norm,
                      k_nope_norm, k_rope_norm, cos_t, sin_t)
    o = flash_attention(q, k, v, batch=batch)
    return [o], [w_o.astype(BF16)]


def kernel(x, mix_norm_even, w_in_even, conv_w, conv_b, dt_bias, a_log, d_skip, ssd_norm, gla_gk_w2, gla_gk_b, gla_norm, w_out_even, mix_norm_odd, w_dqkv, q_lora_norm, w_uq, kv_lora_norm, w_ukv, q_nope_norm, q_rope_norm, k_nope_norm, k_rope_norm, w_o_mla, ffn_norm, w_gate, w_up, w_down):
    batch, seqlen, d = x.shape
    depth = ffn_norm.shape[0]
    cos_t, sin_t = _rope_tables(seqlen, batch)
    x = x.reshape(batch * seqlen, d)
    for i in range(depth):
        j = i // 2
        if i % 2 == 0:
            acts, w_outs = _even_mixer(x, batch, mix_norm_even[j], w_in_even[j], conv_w[j], conv_b[j], dt_bias[j],
                                       a_log[j], d_skip[j], ssd_norm[j], gla_gk_w2[j], gla_gk_b[j], gla_norm[j],
                                       w_out_even[j])
        else:
            acts, w_outs = _odd_mixer(x, batch, cos_t, sin_t, mix_norm_odd[j], w_dqkv[j], q_lora_norm[j], w_uq[j],
                                      kv_lora_norm[j], w_ukv[j], q_nope_norm[j], q_rope_norm[j], k_nope_norm[j],
                                      k_rope_norm[j], w_o_mla[j])
        x = mix_ffn_residual(x, acts, w_outs, ffn_norm[i], w_gate[i].astype(BF16), w_up[i].astype(BF16),
                             w_down[i].astype(BF16))
    return x.reshape(batch, seqlen, d)
```
